```python
import jax, jax.numpy as jnp
from jax import lax
import numpy as np

D_MODEL = 4096
BATCH = 4
SEQ = 4096
DEPTH = 4

HEAD_DIM = 128
N_HEADS_SB = 8
DIL_CONFIGS = ((128, 1), (512, 4), (2048, 16))
N_HEADS_PER_DIL = 4
N_HEADS_DIL = N_HEADS_PER_DIL * len(DIL_CONFIGS)
N_KEYS_DIL = DIL_CONFIGS[0][0] // DIL_CONFIGS[0][1] + 1
W_SB = N_HEADS_SB * HEAD_DIM
W_DIL = N_HEADS_DIL * HEAD_DIM
W_DIL_OUT = N_HEADS_PER_DIL * HEAD_DIM
W_IN = 3 * W_SB + 3 * W_DIL + 2 * D_MODEL
D_FF = -(-8 * D_MODEL // (3 * 256)) * 256
D_PLE = 256
Q_BLOCK = 128
EPS = 1e-6

kernel_name = "hybrid_stickbreak_dilated_gated_block"


def rmsnorm(x, g):
    xf = x.astype(jnp.float32)
    y = xf * lax.rsqrt(jnp.mean(xf * xf, axis=-1, keepdims=True) + EPS)
    return (y * g.astype(jnp.float32)).astype(x.dtype)


def alibi_slopes(n):
    return jnp.exp2(-8.0 * jnp.arange(1, n + 1, dtype=jnp.float32) / n)


def stick_breaking_attention(q, k, v):
    b, s, h, dh = q.shape
    scale = dh ** -0.5
    kpos = jnp.arange(s)

    def block(i):
        t0 = i * Q_BLOCK
        qb = lax.dynamic_slice_in_dim(q, t0, Q_BLOCK, axis=1)
        z = jnp.einsum('bqhd,bshd->bhqs', qb, k, preferred_element_type=jnp.float32) * scale
        qpos = t0 + jnp.arange(Q_BLOCK)
        causal = kpos[None, :] < qpos[:, None]
        log_stay = jnp.where(causal, jax.nn.log_sigmoid(-z), 0.0)
        after = lax.cumsum(log_stay, axis=3, reverse=True) - log_stay
        w = jnp.where(causal, jnp.exp(jax.nn.log_sigmoid(z) + after), 0.0)
        return jnp.einsum('bhqs,bshd->bqhd', w.astype(v.dtype), v)

    out = lax.map(block, jnp.arange(s // Q_BLOCK))
    return jnp.moveaxis(out, 0, 1).reshape(b, s, h, dh)


def dilated_attention(q, k, v, slopes):
    b, s, _, dh = q.shape
    n_g = len(DIL_CONFIGS)
    hg = N_HEADS_PER_DIL
    q = q.reshape(b, s, n_g, hg, dh)
    k = k.reshape(b, s, n_g, hg, dh)
    v = v.reshape(b, s, n_g, hg, dh)
    scale = dh ** -0.5
    offs = jnp.arange(N_KEYS_DIL)

    def block(i):
        t0 = i * Q_BLOCK
        qpos = t0 + jnp.arange(Q_BLOCK)
        outs, lses = [], []
        for gi, (window, dil) in enumerate(DIL_CONFIGS):
            dist = dil * offs
            idx = qpos[:, None] - dist[None, :]
            valid = idx >= 0
            idx = jnp.maximum(idx, 0)
            qb = lax.dynamic_slice_in_dim(q[:, :, gi], t0, Q_BLOCK, axis=1)
            kg = jnp.take(k[:, :, gi], idx, axis=1)
            vg = jnp.take(v[:, :, gi], idx, axis=1)
            sc = jnp.einsum('bqhd,bqjhd->bhqj', qb, kg, preferred_element_type=jnp.float32) * scale
            sc = sc - slopes[:, None, None] * dist.astype(jnp.float32)[None, None, :]
            sc = jnp.where(valid, sc, -jnp.inf)
            m = jnp.max(sc, axis=-1, keepdims=True)
            e = jnp.exp(sc - m)
            den = jnp.sum(e, axis=-1, keepdims=True)
            outs.append(jnp.einsum('bhqj,bqjhd->bqhd', e / den, vg.astype(jnp.float32)))
            lses.append(m[..., 0] + jnp.log(den[..., 0]))
        alpha = jax.nn.softmax(jnp.stack(lses, 0), axis=0)
        alpha = jnp.transpose(alpha, (0, 1, 3, 2))[..., None]
        return jnp.sum(alpha * jnp.stack(outs, 0), axis=0).astype(v.dtype)

    out = lax.map(block, jnp.arange(s // Q_BLOCK))
    return jnp.moveaxis(out, 0, 1).reshape(b, s, hg * dh)


def setup_inputs(seed: int = 0) -> dict:
    key = jax.random.key(seed)
    ks = jax.random.split(key, 20)
    f32 = jnp.float32

    def w(k, shape, fan_in):
        return jax.random.normal(k, shape, f32) * (fan_in ** -0.5)

    def gain(k):
        return 1.0 + 0.02 * jax.random.normal(k, (DEPTH, D_MODEL), f32)

    return {
        "x": jax.random.normal(ks[0], (BATCH, SEQ, D_MODEL), f32),
        "p": jax.random.normal(ks[1], (DEPTH, BATCH, SEQ, D_PLE), f32),
        "w_in": w(ks[2], (DEPTH, D_MODEL, W_IN), D_MODEL),
        "w_proj_sb": w(ks[3], (DEPTH, W_SB, D_MODEL), W_SB),
        "w_proj_dil": w(ks[4], (DEPTH, W_DIL_OUT, D_MODEL), W_DIL_OUT),
        "w_out": w(ks[5], (DEPTH, D_MODEL, D_MODEL), D_MODEL),
        "g_mix_pre": gain(ks[6]),
        "g_mix_post": gain(ks[7]),
        "w_ffn_gate": w(ks[8], (DEPTH, D_MODEL, D_FF), D_MODEL),
        "w_ffn_up": w(ks[9], (DEPTH, D_MODEL, D_FF), D_MODEL),
        "w_ffn_down": w(ks[10], (DEPTH, D_FF, D_MODEL), D_FF),
        "g_ffn_pre": gain(ks[11]),
        "g_ffn_post": gain(ks[12]),
        "w_ple_in": w(ks[13], (DEPTH, D_PLE, D_MODEL), D_PLE),
        "w_ple_gate_down": w(ks[14], (DEPTH, D_MODEL, D_PLE), D_MODEL),
        "w_ple_gate_up": w(ks[15], (DEPTH, D_PLE, D_MODEL), D_PLE),
        "g_ple_gate": gain(ks[16]),
        "g_ple_post": gain(ks[17]),
    }


def reference(x, p, w_in, w_proj_sb, w_proj_dil, w_out, g_mix_pre, g_mix_post,
              w_ffn_gate, w_ffn_up, w_ffn_down, g_ffn_pre, g_ffn_post,
              w_ple_in, w_ple_gate_down, w_ple_gate_up, g_ple_gate, g_ple_post):
    b, s, _ = x.shape
    slopes = alibi_slopes(N_HEADS_PER_DIL)
    splits = [W_SB, 2 * W_SB, 3 * W_SB, 3 * W_SB + W_DIL, 3 * W_SB + 2 * W_DIL,
              3 * W_SB + 3 * W_DIL, 3 * W_SB + 3 * W_DIL + D_MODEL]
    h = x
    for i in range(DEPTH):
        xn = rmsnorm(h, g_mix_pre[i])
        proj = xn @ w_in[i]
        q_sb, k_sb, v_sb, q_d, k_d, v_d, gate_sb, gate_d = jnp.split(proj, splits, axis=-1)
        hs = lambda t, n: t.reshape(b, s, n, HEAD_DIM)
        o_sb = stick_breaking_attention(hs(q_sb, N_HEADS_SB), hs(k_sb, N_HEADS_SB),
                                        hs(v_sb, N_HEADS_SB)).reshape(b, s, W_SB)
        o_d = dilated_attention(hs(q_d, N_HEADS_DIL), hs(k_d, N_HEADS_DIL),
                                hs(v_d, N_HEADS_DIL), slopes)
        merged = (jax.nn.sigmoid(gate_sb) * (o_sb @ w_proj_sb[i])
                  + jax.nn.sigmoid(gate_d) * (o_d @ w_proj_dil[i]))
        h = h + rmsnorm(merged @ w_out[i], g_mix_post[i])
        xn = rmsnorm(h, g_ffn_pre[i])
        f = (jax.nn.silu(xn @ w_ffn_gate[i]) * (xn @ w_ffn_up[i])) @ w_ffn_down[i]
        h = h + rmsnorm(f, g_ffn_post[i])
        gate = jax.nn.sigmoid((rmsnorm(h, g_ple_gate[i]) @ w_ple_gate_down[i]) @ w_ple_gate_up[i])
        e = (p[i] @ w_ple_in[i]) * gate
        h = h + rmsnorm(e, g_ple_post[i])
    return h
```

```python
import functools

import jax
import jax.numpy as jnp
from jax import lax
from jax.experimental import pallas as pl
from jax.experimental.pallas import tpu as pltpu

HEAD_DIM = 128
N_HEADS_SB = 8
DIL_CONFIGS = ((128, 1), (512, 4), (2048, 16))
N_HEADS_PER_DIL = 4
EPS = 1e-6

W_SB = N_HEADS_SB * HEAD_DIM
W_DIL_GROUP = N_HEADS_PER_DIL * HEAD_DIM
W_DIL = W_DIL_GROUP * len(DIL_CONFIGS)
W_QKV = 3 * W_SB + 3 * W_DIL
ATT_TILE = DIL_CONFIGS[0][0] // DIL_CONFIGS[0][1]
assert all(w // r == ATT_TILE for w, r in DIL_CONFIGS)
assert ATT_TILE == HEAD_DIM

F32_EXP_UNDERFLOW = 104.0

V7X_VMEM_BYTES = 64 * 1024 * 1024
VMEM_LIMIT_BYTES = V7X_VMEM_BYTES - 6 * 1024 * 1024

BF16 = jnp.bfloat16
F32 = jnp.float32


def _params(*sem):
    return pltpu.CompilerParams(dimension_semantics=sem, vmem_limit_bytes=VMEM_LIMIT_BYTES)


def _pick(n, candidates):
    for c in candidates:
        if n % c == 0:
            return c
    raise ValueError(f"no tile in {candidates} divides {n}")


def _matmul_tiles(m, k, n, out_bytes, n_weights=1, col0=0):
    budget = VMEM_LIMIT_BYTES - 8 * 1024 * 1024
    for tm, tn in ((1024, 1024), (1024, 768), (1024, 512), (1024, 256), (512, 512), (512, 256),
                   (256, 256), (256, 128), (128, 128)):
        if m % tm or n % tn or col0 % tn:
            continue
        need = 2 * (tm * k * 2 + n_weights * k * tn * 2 + tm * tn * out_bytes) + n_weights * tm * tn * 4
        if need <= budget:
            return tm, tn
    raise ValueError(f"no matmul tiling for {(m, k, n)}")


def _rms(x, g):
    return x * lax.rsqrt(jnp.mean(x * x, axis=-1, keepdims=True) + EPS) * g


def _rmsnorm_kernel(x_ref, g_ref, o_ref):
    o_ref[...] = _rms(x_ref[...], g_ref[...]).astype(o_ref.dtype)


def rmsnorm_bf16(x, g, layer):
    m, d = x.shape
    tm = _pick(m, (512, 256, 128, 8))
    return pl.pallas_call(
        _rmsnorm_kernel,
        grid=(m // tm,),
        in_specs=[pl.BlockSpec((tm, d), lambda i: (i, 0)),
                  pl.BlockSpec((None, 1, d), lambda i: (layer, 0, 0))],
        out_specs=pl.BlockSpec((tm, d), lambda i: (i, 0)),
        out_shape=jax.ShapeDtypeStruct((m, d), BF16),
        compiler_params=_params("parallel"),
        name="rmsnorm",
    )(x, g)


def _matmul_kernel(x_ref, w_ref, o_ref, *, sigmoid):
    acc = jnp.dot(x_ref[...], w_ref[...], preferred_element_type=F32)
    if sigmoid:
        acc = jax.nn.sigmoid(acc)
    o_ref[...] = acc.astype(o_ref.dtype)


def matmul(x, w, layer, *, col0, n_out, out_dtype, sigmoid=False, name):
    m, k = x.shape
    tm, tn = _matmul_tiles(m, k, n_out, jnp.dtype(out_dtype).itemsize, col0=col0)
    c0 = col0 // tn
    return pl.pallas_call(
        functools.partial(_matmul_kernel, sigmoid=sigmoid),
        grid=(m // tm, n_out // tn),
        in_specs=[pl.BlockSpec((tm, k), lambda i, j: (i, 0)),
                  pl.BlockSpec((None, k, tn), lambda i, j: (layer, 0, j + c0))],
        out_specs=pl.BlockSpec((tm, tn), lambda i, j: (i, j)),
        out_shape=jax.ShapeDtypeStruct((m, n_out), out_dtype),
        compiler_params=_params("parallel", "parallel"),
        name=name,
    )(x, w)


def _swiglu_kernel(x_ref, wg_ref, wu_ref, o_ref):
    x = x_ref[...]
    g = jnp.dot(x, wg_ref[...], preferred_element_type=F32)
    u = jnp.dot(x, wu_ref[...], preferred_element_type=F32)
    o_ref[...] = (g * jax.nn.sigmoid(g) * u).astype(o_ref.dtype)


def swiglu_hidden(x, wg, wu, layer):
    m, k = x.shape
    n = wg.shape[-1]
    tm, tn = _matmul_tiles(m, k, n, 2, n_weights=2)
    return pl.pallas_call(
        _swiglu_kernel,
        grid=(m // tm, n // tn),
        in_specs=[pl.BlockSpec((tm, k), lambda i, j: (i, 0)),
                  pl.BlockSpec((None, k, tn), lambda i, j: (layer, 0, j)),
                  pl.BlockSpec((None, k, tn), lambda i, j: (layer, 0, j))],
        out_specs=pl.BlockSpec((tm, tn), lambda i, j: (i, j)),
        out_shape=jax.ShapeDtypeStruct((m, n), BF16),
        compiler_params=_params("parallel", "parallel"),
        name="swiglu_hidden",
    )(x, wg, wu)


def _norm_residual_kernel(y_ref, h_ref, gpost_ref, gnext_ref, h_out_ref, xn_out_ref):
    h = h_ref[...] + _rms(y_ref[...], gpost_ref[...])
    h_out_ref[...] = h
    xn_out_ref[...] = _rms(h, gnext_ref[...]).astype(xn_out_ref.dtype)


def norm_residual(y, h, g_post, g_next, layer, next_layer):
    m, d = h.shape
    tm = _pick(m, (256, 128, 8))
    row = pl.BlockSpec((tm, d), lambda i: (i, 0))
    return pl.pallas_call(
        _norm_residual_kernel,
        grid=(m // tm,),
        in_specs=[row, row,
                  pl.BlockSpec((None, 1, d), lambda i: (layer, 0, 0)),
                  pl.BlockSpec((None, 1, d), lambda i: (next_layer, 0, 0))],
        out_specs=[row, row],
        out_shape=[jax.ShapeDtypeStruct((m, d), F32), jax.ShapeDtypeStruct((m, d), BF16)],
        compiler_params=_params("parallel"),
        name="norm_residual",
    )(y, h, g_post, g_next)


def _sb_attention_kernel(q_ref, k_ref, v_ref, tri_ref, o_ref):
    s_len = q_ref.shape[0]
    t = ATT_TILE
    scale = HEAD_DIM ** -0.5
    row = lax.broadcasted_iota(jnp.int32, (t, t), 0)
    col = lax.broadcasted_iota(jnp.int32, (t, t), 1)
    tri = tri_ref[...]

    def q_tile(i, carry):
        q = q_ref[pl.ds(pl.multiple_of(i * t, t), t), :]

        def k_tile(state):
            j, _, run, acc = state
            k0 = pl.multiple_of(j * t, t)
            k = k_ref[pl.ds(k0, t), :]
            v = v_ref[pl.ds(k0, t), :]
            z = lax.dot_general(q, k, (((1,), (1,)), ((), ())), preferred_element_type=F32) * scale
            causal = (col + j * t) < (row + i * t)
            log_beta = jnp.minimum(z, 0.0) - jnp.log1p(jnp.exp(-jnp.abs(z)))
            log_stay = jnp.where(causal, log_beta - z, 0.0)
            hi = log_stay.astype(BF16)
            lo = (log_stay - hi.astype(F32)).astype(BF16)
            sums = (jnp.dot(hi, tri, preferred_element_type=F32)
                    + jnp.dot(lo, tri, preferred_element_type=F32))
            after = run + sums[:, :t]
            w = jnp.where(causal, jnp.exp(log_beta + after), 0.0)
            acc = acc + jnp.dot(w.astype(BF16), v, preferred_element_type=F32)
            run = run + sums[:, t:]
            more = jnp.logical_and(j > 0, jnp.max(run) > -F32_EXP_UNDERFLOW)
            return j - 1, more, run, acc

        init = (i, True, jnp.zeros((t, t), F32), jnp.zeros((t, HEAD_DIM), F32))
        _, _, _, acc = lax.while_loop(lambda st: st[1], k_tile, init)
        o_ref[pl.ds(pl.multiple_of(i * t, t), t), :] = acc.astype(o_ref.dtype)
        return carry

    lax.fori_loop(0, s_len // t, q_tile, 0)


def _suffix_sum_matrix():
    t = ATT_TILE
    r = jnp.arange(t)
    strict_lower = (r[:, None] > r[None, :]).astype(BF16)
    return jnp.concatenate([strict_lower, jnp.ones((t, t), BF16)], axis=1)


def sb_attention(qkv, batch, seq):
    m = qkv.shape[0]
    h = N_HEADS_SB
    blk = lambda off: pl.BlockSpec((seq, HEAD_DIM), lambda b, n: (b, n + off))
    return pl.pallas_call(
        _sb_attention_kernel,
        grid=(batch, h),
        in_specs=[blk(0), blk(h), blk(2 * h),
                  pl.BlockSpec((ATT_TILE, 2 * ATT_TILE), lambda b, n: (0, 0))],
        out_specs=blk(0),
        out_shape=jax.ShapeDtypeStruct((m, W_SB), BF16),
        compiler_params=_params("parallel", "parallel"),
        name="sb_attention",
    )(qkv, qkv, qkv, _suffix_sum_matrix())


def _dil_attention_kernel(q_ref, kp_ref, kc_ref, vp_ref, vc_ref, o_ref, lse_ref, *, dilation):
    t = ATT_TILE
    scale = HEAD_DIM ** -0.5
    i = pl.program_id(2)
    row = lax.broadcasted_iota(jnp.int32, (t, t), 0)
    col = lax.broadcasted_iota(jnp.int32, (t, t), 1)
    valid_p = jnp.logical_and(col >= row, i > 0)
    valid_c = col <= row
    dist_p = ((t + row - col) * dilation).astype(F32)
    dist_c = ((row - col) * dilation).astype(F32)
    nt = (((1,), (1,)), ((), ()))
    for hh in range(N_HEADS_PER_DIL):
        slope = 2.0 ** (-8.0 * (hh + 1) / N_HEADS_PER_DIL)
        sl = slice(hh * HEAD_DIM, (hh + 1) * HEAD_DIM)
        q = q_ref[:, sl]
        sp = lax.dot_general(q, kp_ref[:, sl], nt, preferred_element_type=F32) * scale
        sc = lax.dot_general(q, kc_ref[:, sl], nt, preferred_element_type=F32) * scale
        sp = jnp.where(valid_p, sp - slope * dist_p, -jnp.inf)
        sc = jnp.where(valid_c, sc - slope * dist_c, -jnp.inf)
        mx = jnp.maximum(jnp.max(sp, axis=1, keepdims=True), jnp.max(sc, axis=1, keepdims=True))
        ep = jnp.exp(sp - mx)
        ec = jnp.exp(sc - mx)
        den = jnp.sum(ep, axis=1, keepdims=True) + jnp.sum(ec, axis=1, keepdims=True)
        pv = (jnp.dot(ep.astype(BF16), vp_ref[:, sl], preferred_element_type=F32)
              + jnp.dot(ec.astype(BF16), vc_ref[:, sl], preferred_element_type=F32))
        o_ref[:, sl] = pv / den
        lse_ref[:, sl] = jnp.broadcast_to(mx + jnp.log(den), (t, HEAD_DIM))


def dil_attention_group(qkv, batch, seq, group):
    m, wq = qkv.shape
    _, r = DIL_CONFIGS[group]
    length = seq // r
    t = ATT_TILE
    nb = length // t
    w = W_DIL_GROUP
    assert wq % w == 0 and length % t == 0
    cols = wq // w
    q_col = (3 * W_SB) // w + group
    k_col = q_col + W_DIL // w
    v_col = k_col + W_DIL // w
    strided = qkv.reshape(m // r, r * wq)

    def spec(col, prev):
        if prev:
            return pl.BlockSpec((t, w), lambda b, c, i: (b * nb + jnp.maximum(i - 1, 0), c * cols + col))
        return pl.BlockSpec((t, w), lambda b, c, i: (b * nb + i, c * cols + col))

    out_spec = pl.BlockSpec((t, w), lambda b, c, i: (b * nb + i, c))
    o, lse = pl.pallas_call(
        functools.partial(_dil_attention_kernel, dilation=r),
        grid=(batch, r, nb),
        in_specs=[spec(q_col, False), spec(k_col, True), spec(k_col, False),
                  spec(v_col, True), spec(v_col, False)],
        out_specs=[out_spec, out_spec],
        out_shape=[jax.ShapeDtypeStruct((m // r, r * w), F32)] * 2,
        compiler_params=_params("parallel", "parallel", "arbitrary"),
        name=f"dil_attention_r{r}",
    )(strided, strided, strided, strided, strided)
    return o.reshape(m, w), lse.reshape(m, w)


def _dil_merge_kernel(*refs):
    n = len(DIL_CONFIGS)
    o_refs, lse_refs, out_ref = refs[:n], refs[n:2 * n], refs[2 * n]
    lses = [r[...] for r in lse_refs]
    mx = functools.reduce(jnp.maximum, lses)
    ws = [jnp.exp(l - mx) for l in lses]
    num = sum(w * o[...] for w, o in zip(ws, o_refs))
    out_ref[...] = (num / sum(ws)).astype(out_ref.dtype)


def dil_merge(outs, lses):
    m, w = outs[0].shape
    tm = _pick(m, (1024, 512, 256, 128, 8))
    row = pl.BlockSpec((tm, w), lambda i: (i, 0))
    return pl.pallas_call(
        _dil_merge_kernel,
        grid=(m // tm,),
        in_specs=[row] * (2 * len(outs)),
        out_specs=row,
        out_shape=jax.ShapeDtypeStruct((m, w), BF16),
        compiler_params=_params("parallel"),
        name="dil_merge",
    )(*outs, *lses)


def _gated_proj_kernel(osb_ref, od_ref, psb_ref, pd_ref, gsb_ref, gd_ref, o_ref):
    a = jnp.dot(osb_ref[...], psb_ref[...], preferred_element_type=F32)
    b = jnp.dot(od_ref[...], pd_ref[...], preferred_element_type=F32)
    o_ref[...] = (gsb_ref[...].astype(F32) * a + gd_ref[...].astype(F32) * b).astype(o_ref.dtype)


def gated_proj(o_sb, o_d, p_sb, p_d, gates, layer):
    m = o_sb.shape[0]
    d = p_sb.shape[-1]
    tm = _pick(m, (1024, 512, 256, 128))
    tn = _pick(d, (1024, 512, 256, 128))
    nd = d // tn
    return pl.pallas_call(
        _gated_proj_kernel,
        grid=(m // tm, nd),
        in_specs=[pl.BlockSpec((tm, o_sb.shape[1]), lambda i, j: (i, 0)),
                  pl.BlockSpec((tm, o_d.shape[1]), lambda i, j: (i, 0)),
                  pl.BlockSpec((None, p_sb.shape[1], tn), lambda i, j: (layer, 0, j)),
                  pl.BlockSpec((None, p_d.shape[1], tn), lambda i, j: (layer, 0, j)),
                  pl.BlockSpec((tm, tn), lambda i, j: (i, j)),
                  pl.BlockSpec((tm, tn), lambda i, j: (i, j + nd))],
        out_specs=pl.BlockSpec((tm, tn), lambda i, j: (i, j)),
        out_shape=jax.ShapeDtypeStruct((m, d), BF16),
        compiler_params=_params("parallel", "parallel"),
        name="gated_proj",
    )(o_sb, o_d, p_sb, p_d, gates, gates)


def _ple_kernel(xn_ref, p_ref, h_ref, gd_ref, gu_ref, wp_ref, gpost_ref, gnext_ref,
                h_out_ref, xn_out_ref):
    low = jnp.dot(xn_ref[...], gd_ref[...], preferred_element_type=F32)
    gate = jax.nn.sigmoid(jnp.dot(low.astype(BF16), gu_ref[...], preferred_element_type=F32))
    e = jnp.dot(p_ref[...].astype(BF16), wp_ref[...], preferred_element_type=F32) * gate
    h = h_ref[...] + _rms(e, gpost_ref[...])
    h_out_ref[...] = h
    xn_out_ref[...] = _rms(h, gnext_ref[...]).astype(xn_out_ref.dtype)


def ple_block(xn, p, h, g_down, g_up, w_ple, g_post, g_next, layer, next_layer):
    m, d = h.shape
    dp = p.shape[-1]
    tm = _pick(m, (256, 128, 8))
    row = pl.BlockSpec((tm, d), lambda i: (i, 0))
    gain = lambda l: pl.BlockSpec((None, 1, d), lambda i: (l, 0, 0))
    return pl.pallas_call(
        _ple_kernel,
        grid=(m // tm,),
        in_specs=[row,
                  pl.BlockSpec((None, tm, dp), lambda i: (layer, i, 0)),
                  row,
                  pl.BlockSpec((None, d, dp), lambda i: (layer, 0, 0)),
                  pl.BlockSpec((None, dp, d), lambda i: (layer, 0, 0)),
                  pl.BlockSpec((None, dp, d), lambda i: (layer, 0, 0)),
                  gain(layer), gain(next_layer)],
        out_specs=[row, row],
        out_shape=[jax.ShapeDtypeStruct((m, d), F32), jax.ShapeDtypeStruct((m, d), BF16)],
        compiler_params=_params("parallel"),
        name="ple_block",
    )(xn, p, h, g_down, g_up, w_ple, g_post, g_next)


def kernel(x, p, w_in, w_proj_sb, w_proj_dil, w_out, g_mix_pre, g_mix_post,
           w_ffn_gate, w_ffn_up, w_ffn_down, g_ffn_pre, g_ffn_post,
           w_ple_in, w_ple_gate_down, w_ple_gate_up, g_ple_gate, g_ple_post):
    b, s, d = x.shape
    depth = w_in.shape[0]
    m = b * s
    assert w_in.shape[-1] == W_QKV + 2 * d

    bf = lambda w: w.astype(BF16)
    w_in, w_proj_sb, w_proj_dil, w_out = bf(w_in), bf(w_proj_sb), bf(w_proj_dil), bf(w_out)
    w_ffn_gate, w_ffn_up, w_ffn_down = bf(w_ffn_gate), bf(w_ffn_up), bf(w_ffn_down)
    w_ple_in, w_ple_gate_down, w_ple_gate_up = bf(w_ple_in), bf(w_ple_gate_down), bf(w_ple_gate_up)
    gains = lambda g: g.reshape(depth, 1, d)
    g_mix_pre, g_mix_post, g_ffn_pre, g_ffn_post, g_ple_gate, g_ple_post = map(
        gains, (g_mix_pre, g_mix_post, g_ffn_pre, g_ffn_post, g_ple_gate, g_ple_post))
    p = p.reshape(depth, m, p.shape[-1])

    h = x.reshape(m, d)
    xn = rmsnorm_bf16(h, g_mix_pre, 0)
    for i in range(depth):
        qkv = matmul(xn, w_in, i, col0=0, n_out=W_QKV, out_dtype=BF16, name="in_proj_qkv")
        gates = matmul(xn, w_in, i, col0=W_QKV, n_out=2 * d, out_dtype=BF16, sigmoid=True,
                       name="in_proj_gates")
        o_sb = sb_attention(qkv, b, s)
        groups = [dil_attention_group(qkv, b, s, g) for g in range(len(DIL_CONFIGS))]
        o_d = dil_merge([o for o, _ in groups], [l for _, l in groups])
        merged = gated_proj(o_sb, o_d, w_proj_sb, w_proj_dil, gates, i)
        y = matmul(merged, w_out, i, col0=0, n_out=d, out_dtype=F32, name="out_proj")
        h, xn = norm_residual(y, h, g_mix_post, g_ffn_pre, i, i)
        hidden = swiglu_hidden(xn, w_ffn_gate, w_ffn_up, i)
        y = matmul(hidden, w_ffn_down, i, col0=0, n_out=d, out_dtype=F32, name="ffn_down")
        h, xn = norm_residual(y, h, g_ffn_post, g_ple_gate, i, i)
        h, xn = ple_block(xn, p, h, w_ple_gate_down, w_ple_gate_up, w_ple_in,
                          g_ple_post, g_mix_pre, i, (i + 1) % depth)
    return h.reshape(b, s, d)
```

```python
import functools

import jax
import jax.numpy as jnp
from jax import lax
from jax.experimental import pallas as pl
from jax.experimental.pallas import tpu as pltpu

HEAD_DIM = 128
N_HEADS_SB = 8
DIL_CONFIGS = ((128, 1), (512, 4), (2048, 16))
N_HEADS_PER_DIL = 4
EPS = 1e-6

W_SB = N_HEADS_SB * HEAD_DIM
W_DIL_GROUP = N_HEADS_PER_DIL * HEAD_DIM
W_DIL = W_DIL_GROUP * len(DIL_CONFIGS)
W_QKV = 3 * W_SB + 3 * W_DIL
ATT_TILE = DIL_CONFIGS[0][0] // DIL_CONFIGS[0][1]
assert all(w // r == ATT_TILE for w, r in DIL_CONFIGS)
assert ATT_TILE == HEAD_DIM

F32_EXP_UNDERFLOW = 104.0
SB_WINDOW_TILES = 3
SB_Q_TILES_PER_STEP = 4

V7X_VMEM_BYTES = 64 * 1024 * 1024
VMEM_LIMIT_BYTES = V7X_VMEM_BYTES - 6 * 1024 * 1024

BF16 = jnp.bfloat16
F32 = jnp.float32


def _params(*sem):
    return pltpu.CompilerParams(dimension_semantics=sem, vmem_limit_bytes=VMEM_LIMIT_BYTES)


def _pick(n, candidates):
    for c in candidates:
        if n % c == 0:
            return c
    raise ValueError(f"no tile in {candidates} divides {n}")


def _matmul_tiles(m, k, n, out_bytes, n_weights=1, col0=0):
    budget = VMEM_LIMIT_BYTES - 8 * 1024 * 1024
    for tm, tn in ((1024, 1280), (1024, 1024), (2048, 512), (1024, 768), (2048, 256), (1024, 512),
                   (1024, 256), (512, 512), (512, 256), (256, 256), (256, 128), (128, 128)):
        if m % tm or n % tn or col0 % tn:
            continue
        need = 2 * (tm * k * 2 + n_weights * k * tn * 2 + tm * tn * out_bytes) + n_weights * tm * tn * 4
        if need <= budget:
            return tm, tn
    raise ValueError(f"no matmul tiling for {(m, k, n)}")


def _rms(x, g):
    return x * lax.rsqrt(jnp.mean(x * x, axis=-1, keepdims=True) + EPS) * g


def _rmsnorm_kernel(x_ref, g_ref, o_ref):
    o_ref[...] = _rms(x_ref[...], g_ref[...]).astype(o_ref.dtype)


def rmsnorm_bf16(x, g, layer):
    m, d = x.shape
    tm = _pick(m, (512, 256, 128, 8))
    return pl.pallas_call(
        _rmsnorm_kernel,
        grid=(m // tm,),
        in_specs=[pl.BlockSpec((tm, d), lambda i: (i, 0)),
                  pl.BlockSpec((None, 1, d), lambda i: (layer, 0, 0))],
        out_specs=pl.BlockSpec((tm, d), lambda i: (i, 0)),
        out_shape=jax.ShapeDtypeStruct((m, d), BF16),
        compiler_params=_params("parallel"),
        name="rmsnorm",
    )(x, g)


def _matmul_kernel(x_ref, w_ref, o_ref, *, sigmoid):
    acc = jnp.dot(x_ref[...], w_ref[...], preferred_element_type=F32)
    if sigmoid:
        acc = jax.nn.sigmoid(acc)
    o_ref[...] = acc.astype(o_ref.dtype)


def matmul(x, w, layer, *, col0, n_out, out_dtype, sigmoid=False, name):
    m, k = x.shape
    tm, tn = _matmul_tiles(m, k, n_out, jnp.dtype(out_dtype).itemsize, col0=col0)
    c0 = col0 // tn
    return pl.pallas_call(
        functools.partial(_matmul_kernel, sigmoid=sigmoid),
        grid=(m // tm, n_out // tn),
        in_specs=[pl.BlockSpec((tm, k), lambda i, j: (i, 0)),
                  pl.BlockSpec((None, k, tn), lambda i, j: (layer, 0, j + c0))],
        out_specs=pl.BlockSpec((tm, tn), lambda i, j: (i, j)),
        out_shape=jax.ShapeDtypeStruct((m, n_out), out_dtype),
        compiler_params=_params("parallel", "parallel"),
        name=name,
    )(x, w)


def _swiglu_kernel(x_ref, wg_ref, wu_ref, o_ref):
    x = x_ref[...]
    g = jnp.dot(x, wg_ref[...], preferred_element_type=F32)
    u = jnp.dot(x, wu_ref[...], preferred_element_type=F32)
    o_ref[...] = (g * jax.nn.sigmoid(g) * u).astype(o_ref.dtype)


def swiglu_hidden(x, wg, wu, layer):
    m, k = x.shape
    n = wg.shape[-1]
    tm, tn = _matmul_tiles(m, k, n, 2, n_weights=2)
    return pl.pallas_call(
        _swiglu_kernel,
        grid=(m // tm, n // tn),
        in_specs=[pl.BlockSpec((tm, k), lambda i, j: (i, 0)),
                  pl.BlockSpec((None, k, tn), lambda i, j: (layer, 0, j)),
                  pl.BlockSpec((None, k, tn), lambda i, j: (layer, 0, j))],
        out_specs=pl.BlockSpec((tm, tn), lambda i, j: (i, j)),
        out_shape=jax.ShapeDtypeStruct((m, n), BF16),
        compiler_params=_params("parallel", "parallel"),
        name="swiglu_hidden",
    )(x, wg, wu)


def _norm_residual_kernel(y_ref, h_ref, gpost_ref, gnext_ref, h_out_ref, xn_out_ref):
    h = h_ref[...] + _rms(y_ref[...], gpost_ref[...])
    h_out_ref[...] = h
    xn_out_ref[...] = _rms(h, gnext_ref[...]).astype(xn_out_ref.dtype)


def norm_residual(y, h, g_post, g_next, layer, next_layer):
    m, d = h.shape
    tm = _pick(m, (256, 128, 8))
    row = pl.BlockSpec((tm, d), lambda i: (i, 0))
    return pl.pallas_call(
        _norm_residual_kernel,
        grid=(m // tm,),
        in_specs=[row, row,
                  pl.BlockSpec((None, 1, d), lambda i: (layer, 0, 0)),
                  pl.BlockSpec((None, 1, d), lambda i: (next_layer, 0, 0))],
        out_specs=[row, row],
        out_shape=[jax.ShapeDtypeStruct((m, d), F32), jax.ShapeDtypeStruct((m, d), BF16)],
        compiler_params=_params("parallel"),
        name="norm_residual",
    )(y, h, g_post, g_next)


def _sb_scores(q, k, q0, k0):
    z = lax.dot_general(q, k, (((1,), (1,)), ((), ())), preferred_element_type=F32) * HEAD_DIM ** -0.5
    row = lax.broadcasted_iota(jnp.int32, z.shape, 0)
    col = lax.broadcasted_iota(jnp.int32, z.shape, 1)
    causal = (col + k0) < (row + q0)
    log_beta = jnp.minimum(z, 0.0) - jnp.log1p(jnp.exp(-jnp.abs(z)))
    log_stay = jnp.where(causal, log_beta - z, 0.0)
    return log_beta, log_stay, causal


def _suffix_sums(tiles, tri2):
    t = ATT_TILE
    x = jnp.concatenate(tiles, axis=0) if len(tiles) > 1 else tiles[0]
    hi = x.astype(BF16)
    lo = (x - hi.astype(F32)).astype(BF16)
    sums = jnp.dot(jnp.concatenate([hi, lo], axis=1), tri2, preferred_element_type=F32)
    return [(sums[n * t:(n + 1) * t, :t], sums[n * t:(n + 1) * t, t:]) for n in range(len(tiles))]


def _sb_attention_kernel(q_ref, k_ref, v_ref, tri2_ref, o_ref):
    s_len = q_ref.shape[0]
    t = ATT_TILE
    nw = SB_WINDOW_TILES
    wk = nw * t

    def windows(tiles):
        pre = []
        for i in tiles:
            q0 = pl.multiple_of(i * t, t)
            first = jnp.maximum(i - (nw - 1), 0)
            k0 = pl.multiple_of(first * t, t)
            q = q_ref[pl.ds(q0, t), :]
            pre.append((q, q0, first, k0) + _sb_scores(q, k_ref[pl.ds(k0, wk), :], q0, k0))
        sums = _suffix_sums([p[5][:, d * t:(d + 1) * t] for p in pre for d in range(nw)], tri2_ref[...])
        out = []
        for u, (q, q0, first, k0, log_beta, _, causal) in enumerate(pre):
            after, run = [], jnp.zeros((t, t), F32)
            for d in reversed(range(nw)):
                cum, tot = sums[u * nw + d]
                after.append(cum + run)
                run = run + tot
            after = jnp.concatenate(after[::-1], axis=1)
            w = jnp.where(causal, jnp.exp(log_beta + after), 0.0)
            acc = jnp.dot(w.astype(BF16), v_ref[pl.ds(k0, wk), :], preferred_element_type=F32)
            out.append((q, q0, first - 1, run, acc))
        return out

    def more(j, run):
        return jnp.logical_and(j >= 0, jnp.max(run) > -F32_EXP_UNDERFLOW)

    def tail(q, q0, j, run, acc):
        def k_tile(state):
            j, _, run, acc = state
            k0 = pl.multiple_of(j * t, t)
            log_beta, log_stay, _ = _sb_scores(q, k_ref[pl.ds(k0, t), :], q0, k0)
            (cum, tot), = _suffix_sums([log_stay], tri2_ref[...])
            w = jnp.exp(log_beta + run + cum)
            acc = acc + jnp.dot(w.astype(BF16), v_ref[pl.ds(k0, t), :], preferred_element_type=F32)
            run = run + tot
            return j - 1, more(j - 1, run), run, acc

        return lax.while_loop(lambda st: st[1], k_tile, (j, more(j, run), run, acc))[3]

    def q_group(g, carry):
        wins = windows([g * SB_Q_TILES_PER_STEP + u for u in range(SB_Q_TILES_PER_STEP)])
        for q, q0, j, run, acc in wins:
            o_ref[pl.ds(q0, t), :] = tail(q, q0, j, run, acc).astype(o_ref.dtype)
        return carry

    lax.fori_loop(0, s_len // (t * SB_Q_TILES_PER_STEP), q_group, 0)


def _suffix_sum_matrix():
    r = jnp.arange(ATT_TILE)
    strict_lower = (r[:, None] > r[None, :]).astype(BF16)
    half = jnp.concatenate([strict_lower, jnp.ones((ATT_TILE, ATT_TILE), BF16)], axis=1)
    return jnp.concatenate([half, half], axis=0)


def sb_attention(qkv, batch, seq):
    m = qkv.shape[0]
    h = N_HEADS_SB
    assert seq % (ATT_TILE * SB_Q_TILES_PER_STEP) == 0 and seq >= SB_WINDOW_TILES * ATT_TILE
    blk = lambda off: pl.BlockSpec((seq, HEAD_DIM), lambda b, n: (b, n + off))
    tri2 = _suffix_sum_matrix()
    return pl.pallas_call(
        _sb_attention_kernel,
        grid=(batch, h),
        in_specs=[blk(0), blk(h), blk(2 * h), pl.BlockSpec(tri2.shape, lambda b, n: (0, 0))],
        out_specs=blk(0),
        out_shape=jax.ShapeDtypeStruct((m, W_SB), BF16),
        compiler_params=_params("parallel", "parallel"),
        name="sb_attention",
    )(qkv, qkv, qkv, tri2)


def _dil_attention_kernel(q_ref, kp_ref, kc_ref, vp_ref, vc_ref, o_ref, lse_ref, *, dilation):
    t = ATT_TILE
    scale = HEAD_DIM ** -0.5
    i = pl.program_id(2)
    row = lax.broadcasted_iota(jnp.int32, (t, t), 0)
    col = lax.broadcasted_iota(jnp.int32, (t, t), 1)
    valid_p = jnp.logical_and(col >= row, i > 0)
    valid_c = col <= row
    dist_p = ((t + row - col) * dilation).astype(F32)
    dist_c = ((row - col) * dilation).astype(F32)
    nt = (((1,), (1,)), ((), ()))
    for hh in range(N_HEADS_PER_DIL):
        slope = 2.0 ** (-8.0 * (hh + 1) / N_HEADS_PER_DIL)
        sl = slice(hh * HEAD_DIM, (hh + 1) * HEAD_DIM)
        q = q_ref[:, sl]
        sp = lax.dot_general(q, kp_ref[:, sl], nt, preferred_element_type=F32) * scale
        sc = lax.dot_general(q, kc_ref[:, sl], nt, preferred_element_type=F32) * scale
        sp = jnp.where(valid_p, sp - slope * dist_p, -jnp.inf)
        sc = jnp.where(valid_c, sc - slope * dist_c, -jnp.inf)
        mx = jnp.maximum(jnp.max(sp, axis=1, keepdims=True), jnp.max(sc, axis=1, keepdims=True))
        ep = jnp.exp(sp - mx)
        ec = jnp.exp(sc - mx)
        den = jnp.sum(ep, axis=1, keepdims=True) + jnp.sum(ec, axis=1, keepdims=True)
        pv = (jnp.dot(ep.astype(BF16), vp_ref[:, sl], preferred_element_type=F32)
              + jnp.dot(ec.astype(BF16), vc_ref[:, sl], preferred_element_type=F32))
        o_ref[:, sl] = (pv / den).astype(o_ref.dtype)
        lse_ref[:, sl] = jnp.broadcast_to(mx + jnp.log(den), (t, HEAD_DIM))


def dil_attention_group(qkv, batch, seq, group):
    m, wq = qkv.shape
    _, r = DIL_CONFIGS[group]
    length = seq // r
    t = ATT_TILE
    nb = length // t
    w = W_DIL_GROUP
    assert wq % w == 0 and length % t == 0
    q_col = (3 * W_SB) // w + group
    k_col = q_col + W_DIL // w
    v_col = k_col + W_DIL // w
    if r == 1:
        q_arr = k_arr = v_arr = qkv
        cols = wq // w
    else:
        take = lambda col: qkv[:, col * w:(col + 1) * w].reshape(m // r, r * w)
        q_arr, k_arr, v_arr = take(q_col), take(k_col), take(v_col)
        q_col = k_col = v_col = 0
        cols = 1

    def spec(col, prev):
        if prev:
            return pl.BlockSpec((t, w), lambda b, c, i: (b * nb + jnp.maximum(i - 1, 0), c * cols + col))
        return pl.BlockSpec((t, w), lambda b, c, i: (b * nb + i, c * cols + col))

    out_spec = pl.BlockSpec((t, w), lambda b, c, i: (b * nb + i, c))
    o, lse = pl.pallas_call(
        functools.partial(_dil_attention_kernel, dilation=r),
        grid=(batch, r, nb),
        in_specs=[spec(q_col, False), spec(k_col, True), spec(k_col, False),
                  spec(v_col, True), spec(v_col, False)],
        out_specs=[out_spec, out_spec],
        out_shape=[jax.ShapeDtypeStruct((m // r, r * w), BF16),
                   jax.ShapeDtypeStruct((m // r, r * w), F32)],
        compiler_params=_params("parallel", "parallel", "arbitrary"),
        name=f"dil_attention_r{r}",
    )(q_arr, k_arr, k_arr, v_arr, v_arr)
    return o.reshape(m, w), lse.reshape(m, w)


def _dil_merge_kernel(*refs):
    n = len(DIL_CONFIGS)
    o_refs, lse_refs, out_ref = refs[:n], refs[n:2 * n], refs[2 * n]
    lses = [r[...] for r in lse_refs]
    mx = functools.reduce(jnp.maximum, lses)
    ws = [jnp.exp(l - mx) for l in lses]
    num = sum(w * o[...] for w, o in zip(ws, o_refs))
    out_ref[...] = (num / sum(ws)).astype(out_ref.dtype)


def dil_merge(outs, lses):
    m, w = outs[0].shape
    tm = _pick(m, (1024, 512, 256, 128, 8))
    row = pl.BlockSpec((tm, w), lambda i: (i, 0))
    return pl.pallas_call(
        _dil_merge_kernel,
        grid=(m // tm,),
        in_specs=[row] * (2 * len(outs)),
        out_specs=row,
        out_shape=jax.ShapeDtypeStruct((m, w), BF16),
        compiler_params=_params("parallel"),
        name="dil_merge",
    )(*outs, *lses)


def _gated_proj_kernel(osb_ref, od_ref, psb_ref, pd_ref, gsb_ref, gd_ref, o_ref):
    a = jnp.dot(osb_ref[...], psb_ref[...], preferred_element_type=F32)
    b = jnp.dot(od_ref[...], pd_ref[...], preferred_element_type=F32)
    o_ref[...] = (gsb_ref[...].astype(F32) * a + gd_ref[...].astype(F32) * b).astype(o_ref.dtype)


def gated_proj(o_sb, o_d, p_sb, p_d, gates, layer):
    m = o_sb.shape[0]
    d = p_sb.shape[-1]
    tm = _pick(m, (1024, 512, 256, 128))
    tn = _pick(d, (1024, 512, 256, 128))
    nd = d // tn
    return pl.pallas_call(
        _gated_proj_kernel,
        grid=(m // tm, nd),
        in_specs=[pl.BlockSpec((tm, o_sb.shape[1]), lambda i, j: (i, 0)),
                  pl.BlockSpec((tm, o_d.shape[1]), lambda i, j: (i, 0)),
                  pl.BlockSpec((None, p_sb.shape[1], tn), lambda i, j: (layer, 0, j)),
                  pl.BlockSpec((None, p_d.shape[1], tn), lambda i, j: (layer, 0, j)),
                  pl.BlockSpec((tm, tn), lambda i, j: (i, j)),
                  pl.BlockSpec((tm, tn), lambda i, j: (i, j + nd))],
        out_specs=pl.BlockSpec((tm, tn), lambda i, j: (i, j)),
        out_shape=jax.ShapeDtypeStruct((m, d), BF16),
        compiler_params=_params("parallel", "parallel"),
        name="gated_proj",
    )(o_sb, o_d, p_sb, p_d, gates, gates)


def _ffn_close_ple_kernel(y_ref, p_ref, h_ref, gd_ref, gu_ref, wp_ref,
                          g_ffn_post_ref, g_gate_ref, g_post_ref, g_next_ref, h_out_ref, xn_out_ref):
    h = h_ref[...] + _rms(y_ref[...], g_ffn_post_ref[...])
    xn = _rms(h, g_gate_ref[...]).astype(BF16)
    low = jnp.dot(xn, gd_ref[...], preferred_element_type=F32)
    gate = jax.nn.sigmoid(jnp.dot(low.astype(BF16), gu_ref[...], preferred_element_type=F32))
    e = jnp.dot(p_ref[...].astype(BF16), wp_ref[...], preferred_element_type=F32) * gate
    h = h + _rms(e, g_post_ref[...])
    h_out_ref[...] = h
    xn_out_ref[...] = _rms(h, g_next_ref[...]).astype(xn_out_ref.dtype)


def ffn_close_ple_block(y, p, h, g_down, g_up, w_ple, g_ffn_post, g_gate, g_post, g_next,
                        layer, next_layer):
    m, d = h.shape
    dp = p.shape[-1]
    tm = _pick(m, (256, 128, 8))
    row = pl.BlockSpec((tm, d), lambda i: (i, 0))
    gain = lambda l: pl.BlockSpec((None, 1, d), lambda i: (l, 0, 0))
    return pl.pallas_call(
        _ffn_close_ple_kernel,
        grid=(m // tm,),
        in_specs=[row,
                  pl.BlockSpec((None, tm, dp), lambda i: (layer, i, 0)),
                  row,
                  pl.BlockSpec((None, d, dp), lambda i: (layer, 0, 0)),
                  pl.BlockSpec((None, dp, d), lambda i: (layer, 0, 0)),
                  pl.BlockSpec((None, dp, d), lambda i: (layer, 0, 0)),
                  gain(layer), gain(layer), gain(layer), gain(next_layer)],
        out_specs=[row, row],
        out_shape=[jax.ShapeDtypeStruct((m, d), F32), jax.ShapeDtypeStruct((m, d), BF16)],
        compiler_params=_params("parallel"),
        name="ffn_close_ple",
    )(y, p, h, g_down, g_up, w_ple, g_ffn_post, g_gate, g_post, g_next)


def kernel(x, p, w_in, w_proj_sb, w_proj_dil, w_out, g_mix_pre, g_mix_post,
           w_ffn_gate, w_ffn_up, w_ffn_down, g_ffn_pre, g_ffn_post,
           w_ple_in, w_ple_gate_down, w_ple_gate_up, g_ple_gate, g_ple_post):
    b, s, d = x.shape
    depth = w_in.shape[0]
    m = b * s
    assert w_in.shape[-1] == W_QKV + 2 * d

    bf = lambda w: w.astype(BF16)
    w_in, w_proj_sb, w_proj_dil, w_out = bf(w_in), bf(w_proj_sb), bf(w_proj_dil), bf(w_out)
    w_ffn_gate, w_ffn_up, w_ffn_down = bf(w_ffn_gate), bf(w_ffn_up), bf(w_ffn_down)
    w_ple_in, w_ple_gate_down, w_ple_gate_up = bf(w_ple_in), bf(w_ple_gate_down), bf(w_ple_gate_up)
    gains = lambda g: g.reshape(depth, 1, d)
    g_mix_pre, g_mix_post, g_ffn_pre, g_ffn_post, g_ple_gate, g_ple_post = map(
        gains, (g_mix_pre, g_mix_post, g_ffn_pre, g_ffn_post, g_ple_gate, g_ple_post))
    p = p.reshape(depth, m, p.shape[-1])

    h = x.reshape(m, d)
    xn = rmsnorm_bf16(h, g_mix_pre, 0)
    for i in range(depth):
        qkv = matmul(xn, w_in, i, col0=0, n_out=W_QKV, out_dtype=BF16, name="in_proj_qkv")
        gates = matmul(xn, w_in, i, col0=W_QKV, n_out=2 * d, out_dtype=BF16, sigmoid=True,
                       name="in_proj_gates")
        o_sb = sb_attention(qkv, b, s)
        groups = [dil_attention_group(qkv, b, s, g) for g in range(len(DIL_CONFIGS))]
        o_d = dil_merge([o for o, _ in groups], [l for _, l in groups])
        merged = gated_proj(o_sb, o_d, w_proj_sb, w_proj_dil, gates, i)
        y = matmul(merged, w_out, i, col0=0, n_out=d, out_dtype=F32, name="out_proj")
        h, xn = norm_residual(y, h, g_mix_post, g_ffn_pre, i, i)
        hidden = swiglu_hidden(xn, w_ffn_gate, w_ffn_up, i)
        y = matmul(hidden, w_ffn_down, i, col0=0, n_out=d, out_dtype=F32, name="ffn_down")
        h, xn = ffn_close_ple_block(y, p, h, w_ple_gate_down, w_ple_gate_up, w_ple_in,
                                    g_ffn_post, g_ple_gate, g_ple_post, g_mix_pre, i, (i + 1) % depth)
    return h.reshape(b, s, d)
```

```python
import functools

import jax
import jax.numpy as jnp
from jax import lax
from jax.experimental import pallas as pl
from jax.experimental.pallas import tpu as pltpu

HEAD_DIM = 128
N_HEADS_SB = 8
DIL_CONFIGS = ((128, 1), (512, 4), (2048, 16))
N_HEADS_PER_DIL = 4
EPS = 1e-6

W_SB = N_HEADS_SB * HEAD_DIM
W_DIL_GROUP = N_HEADS_PER_DIL * HEAD_DIM
W_DIL = W_DIL_GROUP * len(DIL_CONFIGS)
W_QKV = 3 * W_SB + 3 * W_DIL
ATT_TILE = DIL_CONFIGS[0][0] // DIL_CONFIGS[0][1]
assert all(w // r == ATT_TILE for w, r in DIL_CONFIGS)
assert ATT_TILE == HEAD_DIM

F32_EXP_UNDERFLOW = 104.0
SB_WINDOW_TILES = 3
SB_Q_TILES_PER_STEP = 4
DIL_TILES_PER_STEP = 2

V7X_VMEM_BYTES = 64 * 1024 * 1024
VMEM_LIMIT_BYTES = V7X_VMEM_BYTES - 6 * 1024 * 1024

BF16 = jnp.bfloat16
F32 = jnp.float32
BF16_SUBLANES = 16


def _params(*sem):
    return pltpu.CompilerParams(dimension_semantics=sem, vmem_limit_bytes=VMEM_LIMIT_BYTES)


def _pick(n, candidates):
    for c in candidates:
        if n % c == 0:
            return c
    raise ValueError(f"no tile in {candidates} divides {n}")


def _matmul_tiles(m, k, n, out_bytes, n_weights=1, col0=0):
    budget = VMEM_LIMIT_BYTES - 8 * 1024 * 1024
    for tm, tn in ((1024, 1280), (1024, 1024), (2048, 512), (1024, 768), (2048, 256), (1024, 512),
                   (1024, 256), (512, 512), (512, 256), (256, 256), (256, 128), (128, 128)):
        if m % tm or n % tn or col0 % tn:
            continue
        need = 2 * (tm * k * 2 + n_weights * k * tn * 2 + tm * tn * out_bytes) + n_weights * tm * tn * 4
        if need <= budget:
            return tm, tn
    raise ValueError(f"no matmul tiling for {(m, k, n)}")


def _rms(x, g):
    return x * lax.rsqrt(jnp.mean(x * x, axis=-1, keepdims=True) + EPS) * g


def _rmsnorm_kernel(x_ref, g_ref, o_ref):
    o_ref[...] = _rms(x_ref[...], g_ref[...]).astype(o_ref.dtype)


def rmsnorm_bf16(x, g, layer):
    m, d = x.shape
    tm = _pick(m, (512, 256, 128, 8))
    return pl.pallas_call(
        _rmsnorm_kernel,
        grid=(m // tm,),
        in_specs=[pl.BlockSpec((tm, d), lambda i: (i, 0)),
                  pl.BlockSpec((None, 1, d), lambda i: (layer, 0, 0))],
        out_specs=pl.BlockSpec((tm, d), lambda i: (i, 0)),
        out_shape=jax.ShapeDtypeStruct((m, d), BF16),
        compiler_params=_params("parallel"),
        name="rmsnorm",
    )(x, g)


def _cast_rows(total_rows, steps):
    for rows in range(BF16_SUBLANES, total_rows + 1, BF16_SUBLANES):
        if total_rows % rows == 0 and total_rows // rows <= steps:
            return rows
    raise ValueError(f"cannot cover {total_rows} rows in {steps} steps")


def _grid_call(body, args, *, grid, in_specs, out_spec, out_shape, name, cast=None):
    if cast is None:
        return pl.pallas_call(
            body, grid=grid, in_specs=in_specs, out_specs=out_spec, out_shape=out_shape,
            compiler_params=_params("parallel", "parallel"), name=name)(*args)
    src, layer = cast
    _, r, c = src.shape
    rows = _cast_rows(r, grid[0] * grid[1])
    blk = lambda i, j: jnp.minimum(i * grid[1] + j, r // rows - 1)
    n_in = len(in_specs)

    def body_with_cast(*refs):
        body(*refs[:n_in], refs[n_in + 1])
        refs[n_in + 2][...] = refs[n_in][...].astype(BF16)

    return pl.pallas_call(
        body_with_cast, grid=grid,
        in_specs=[*in_specs, pl.BlockSpec((None, rows, c), lambda i, j: (layer, blk(i, j), 0))],
        out_specs=[out_spec, pl.BlockSpec((None, rows, c), lambda i, j: (0, blk(i, j), 0))],
        out_shape=[out_shape, jax.ShapeDtypeStruct((1, r, c), BF16)],
        compiler_params=_params("arbitrary", "arbitrary"), name=name)(*args, src)


def _matmul_kernel(x_ref, w_ref, o_ref, *, sigmoid):
    acc = jnp.dot(x_ref[...], w_ref[...], preferred_element_type=F32)
    if sigmoid:
        acc = jax.nn.sigmoid(acc)
    o_ref[...] = acc.astype(o_ref.dtype)


def matmul(x, w, layer, *, col0, n_out, out_dtype, sigmoid=False, name, cast=None):
    m, k = x.shape
    tm, tn = _matmul_tiles(m, k, n_out, jnp.dtype(out_dtype).itemsize, col0=col0)
    c0 = col0 // tn
    return _grid_call(
        functools.partial(_matmul_kernel, sigmoid=sigmoid), (x, w),
        grid=(m // tm, n_out // tn),
        in_specs=[pl.BlockSpec((tm, k), lambda i, j: (i, 0)),
                  pl.BlockSpec((None, k, tn), lambda i, j: (layer, 0, j + c0))],
        out_spec=pl.BlockSpec((tm, tn), lambda i, j: (i, j)),
        out_shape=jax.ShapeDtypeStruct((m, n_out), out_dtype),
        name=name, cast=cast)


def _swiglu_kernel(x_ref, wg_ref, wu_ref, o_ref):
    x = x_ref[...]
    g = jnp.dot(x, wg_ref[...], preferred_element_type=F32)
    u = jnp.dot(x, wu_ref[...], preferred_element_type=F32)
    o_ref[...] = (g * jax.nn.sigmoid(g) * u).astype(o_ref.dtype)


def swiglu_hidden(x, wg, wu, layer, cast=None):
    m, k = x.shape
    n = wg.shape[-1]
    tm, tn = _matmul_tiles(m, k, n, 2, n_weights=2)
    return _grid_call(
        _swiglu_kernel, (x, wg, wu),
        grid=(m // tm, n // tn),
        in_specs=[pl.BlockSpec((tm, k), lambda i, j: (i, 0)),
                  pl.BlockSpec((None, k, tn), lambda i, j: (layer, 0, j)),
                  pl.BlockSpec((None, k, tn), lambda i, j: (layer, 0, j))],
        out_spec=pl.BlockSpec((tm, tn), lambda i, j: (i, j)),
        out_shape=jax.ShapeDtypeStruct((m, n), BF16),
        name="swiglu_hidden", cast=cast)


def _norm_residual_kernel(y_ref, h_ref, gpost_ref, gnext_ref, h_out_ref, xn_out_ref):
    h = h_ref[...] + _rms(y_ref[...], gpost_ref[...])
    h_out_ref[...] = h
    xn_out_ref[...] = _rms(h, gnext_ref[...]).astype(xn_out_ref.dtype)


def norm_residual(y, h, g_post, g_next, layer, next_layer):
    m, d = h.shape
    tm = _pick(m, (256, 128, 8))
    row = pl.BlockSpec((tm, d), lambda i: (i, 0))
    return pl.pallas_call(
        _norm_residual_kernel,
        grid=(m // tm,),
        in_specs=[row, row,
                  pl.BlockSpec((None, 1, d), lambda i: (layer, 0, 0)),
                  pl.BlockSpec((None, 1, d), lambda i: (next_layer, 0, 0))],
        out_specs=[row, row],
        out_shape=[jax.ShapeDtypeStruct((m, d), F32), jax.ShapeDtypeStruct((m, d), BF16)],
        compiler_params=_params("parallel"),
        name="norm_residual",
    )(y, h, g_post, g_next)


def _sb_scores(q, k, q0, k0):
    z = lax.dot_general(q, k, (((1,), (1,)), ((), ())), preferred_element_type=F32) * HEAD_DIM ** -0.5
    row = lax.broadcasted_iota(jnp.int32, z.shape, 0)
    col = lax.broadcasted_iota(jnp.int32, z.shape, 1)
    causal = (col + k0) < (row + q0)
    log_beta = jnp.minimum(z, 0.0) - jnp.log(1.0 + jnp.exp(-jnp.abs(z)))
    log_stay = jnp.where(causal, log_beta - z, 0.0)
    return log_beta, log_stay, causal


def _suffix_sums(tiles, tri2):
    t = ATT_TILE
    x = jnp.concatenate(tiles, axis=0) if len(tiles) > 1 else tiles[0]
    hi = x.astype(BF16)
    lo = (x - hi.astype(F32)).astype(BF16)
    sums = jnp.dot(jnp.concatenate([hi, lo], axis=1), tri2, preferred_element_type=F32)
    return [(sums[n * t:(n + 1) * t, :t], sums[n * t:(n + 1) * t, t:]) for n in range(len(tiles))]


def _sb_attention_kernel(q_ref, k_ref, v_ref, tri2_ref, o_ref):
    s_len = q_ref.shape[0]
    t = ATT_TILE
    nw = SB_WINDOW_TILES
    wk = nw * t

    def windows(tiles):
        pre = []
        for i in tiles:
            q0 = pl.multiple_of(i * t, t)
            first = jnp.maximum(i - (nw - 1), 0)
            k0 = pl.multiple_of(first * t, t)
            q = q_ref[pl.ds(q0, t), :]
            pre.append((q, q0, first, k0) + _sb_scores(q, k_ref[pl.ds(k0, wk), :], q0, k0))
        sums = _suffix_sums([p[5][:, d * t:(d + 1) * t] for p in pre for d in range(nw)], tri2_ref[...])
        out = []
        for u, (q, q0, first, k0, log_beta, _, causal) in enumerate(pre):
            after, run = [], jnp.zeros((t, t), F32)
            for d in reversed(range(nw)):
                cum, tot = sums[u * nw + d]
                after.append(cum + run)
                run = run + tot
            after = jnp.concatenate(after[::-1], axis=1)
            w = jnp.where(causal, jnp.exp(log_beta + after), 0.0)
            acc = jnp.dot(w.astype(BF16), v_ref[pl.ds(k0, wk), :], preferred_element_type=F32)
            out.append((q, q0, first - 1, run, acc))
        return out

    def more(j, run):
        return jnp.logical_and(j >= 0, jnp.max(run) > -F32_EXP_UNDERFLOW)

    def tail(q, q0, j, run, acc):
        def k_tile(state):
            j, _, run, acc = state
            k0 = pl.multiple_of(j * t, t)
            log_beta, log_stay, _ = _sb_scores(q, k_ref[pl.ds(k0, t), :], q0, k0)
            (cum, tot), = _suffix_sums([log_stay], tri2_ref[...])
            w = jnp.exp(log_beta + run + cum)
            acc = acc + jnp.dot(w.astype(BF16), v_ref[pl.ds(k0, t), :], preferred_element_type=F32)
            run = run + tot
            return j - 1, more(j - 1, run), run, acc

        return lax.while_loop(lambda st: st[1], k_tile, (j, more(j, run), run, acc))[3]

    def q_group(g, carry):
        wins = windows([g * SB_Q_TILES_PER_STEP + u for u in range(SB_Q_TILES_PER_STEP)])
        for q, q0, j, run, acc in wins:
            o_ref[pl.ds(q0, t), :] = tail(q, q0, j, run, acc).astype(o_ref.dtype)
        return carry

    lax.fori_loop(0, s_len // (t * SB_Q_TILES_PER_STEP), q_group, 0)


def _suffix_sum_matrix():
    r = jnp.arange(ATT_TILE)
    strict_lower = (r[:, None] > r[None, :]).astype(BF16)
    half = jnp.concatenate([strict_lower, jnp.ones((ATT_TILE, ATT_TILE), BF16)], axis=1)
    return jnp.concatenate([half, half], axis=0)


def sb_attention(qkv, batch, seq):
    m = qkv.shape[0]
    h = N_HEADS_SB
    assert seq % (ATT_TILE * SB_Q_TILES_PER_STEP) == 0 and seq >= SB_WINDOW_TILES * ATT_TILE
    blk = lambda off: pl.BlockSpec((seq, HEAD_DIM), lambda b, n: (b, n + off))
    tri2 = _suffix_sum_matrix()
    return pl.pallas_call(
        _sb_attention_kernel,
        grid=(batch, h),
        in_specs=[blk(0), blk(h), blk(2 * h), pl.BlockSpec(tri2.shape, lambda b, n: (0, 0))],
        out_specs=blk(0),
        out_shape=jax.ShapeDtypeStruct((m, W_SB), BF16),
        compiler_params=_params("parallel", "parallel"),
        name="sb_attention",
    )(qkv, qkv, qkv, tri2)


def _dil_attention_kernel(q_ref, kp_ref, kc_ref, vp_ref, vc_ref, o_ref, lse_ref, *, dilation):
    t = ATT_TILE
    scale = HEAD_DIM ** -0.5
    i = pl.program_id(2)
    row = lax.broadcasted_iota(jnp.int32, (t, 2 * t), 0)
    col = lax.broadcasted_iota(jnp.int32, (t, 2 * t), 1)
    steps = t + row - col
    in_band = jnp.logical_and(steps >= 0, steps <= t)
    dist = (steps * dilation).astype(F32)
    nt = (((1,), (1,)), ((), ()))
    for a in range(q_ref.shape[0] // t):
        rows = slice(a * t, (a + 1) * t)
        prev = slice((a - 1) * t, a * t)
        valid = in_band if a else jnp.logical_and(in_band, jnp.logical_or(col >= t, i > 0))
        for hh in range(N_HEADS_PER_DIL):
            slope = 2.0 ** (-8.0 * (hh + 1) / N_HEADS_PER_DIL)
            sl = slice(hh * HEAD_DIM, (hh + 1) * HEAD_DIM)
            k = jnp.concatenate([kc_ref[prev, sl] if a else kp_ref[:, sl], kc_ref[rows, sl]], axis=0)
            v = jnp.concatenate([vc_ref[prev, sl] if a else vp_ref[:, sl], vc_ref[rows, sl]], axis=0)
            s = lax.dot_general(q_ref[rows, sl], k, nt, preferred_element_type=F32) * scale
            s = jnp.where(valid, s - slope * dist, -jnp.inf)
            mx = jnp.max(s, axis=1, keepdims=True)
            e = jnp.exp(s - mx)
            den = jnp.sum(e, axis=1, keepdims=True)
            pv = jnp.dot(e.astype(BF16), v, preferred_element_type=F32)
            o_ref[rows, sl] = (pv / den).astype(o_ref.dtype)
            lse_ref[rows, sl] = jnp.broadcast_to(mx + jnp.log(den), (t, HEAD_DIM))


def dil_attention_group(qkv, batch, seq, group):
    m, wq = qkv.shape
    _, r = DIL_CONFIGS[group]
    length = seq // r
    t = ATT_TILE
    nb = length // t
    w = W_DIL_GROUP
    assert wq % w == 0 and length % t == 0
    q_col = (3 * W_SB) // w + group
    k_col = q_col + W_DIL // w
    v_col = k_col + W_DIL // w
    if r == 1:
        q_arr = k_arr = v_arr = qkv
        cols = wq // w
    else:
        take = lambda col: qkv[:, col * w:(col + 1) * w].reshape(m // r, r * w)
        q_arr, k_arr, v_arr = take(q_col), take(k_col), take(v_col)
        q_col = k_col = v_col = 0
        cols = 1

    tiles = _pick(nb, (DIL_TILES_PER_STEP, 1))
    ns = nb // tiles

    def spec(col, prev):
        if prev:
            return pl.BlockSpec((t, w), lambda b, c, i: (b * nb + jnp.maximum(i * tiles - 1, 0), c * cols + col))
        return pl.BlockSpec((tiles * t, w), lambda b, c, i: (b * ns + i, c * cols + col))

    out_spec = pl.BlockSpec((tiles * t, w), lambda b, c, i: (b * ns + i, c))
    o, lse = pl.pallas_call(
        functools.partial(_dil_attention_kernel, dilation=r),
        grid=(batch, r, ns),
        in_specs=[spec(q_col, False), spec(k_col, True), spec(k_col, False),
                  spec(v_col, True), spec(v_col, False)],
        out_specs=[out_spec, out_spec],
        out_shape=[jax.ShapeDtypeStruct((m // r, r * w), BF16),
                   jax.ShapeDtypeStruct((m // r, r * w), F32)],
        compiler_params=_params("parallel", "parallel", "arbitrary"),
        name=f"dil_attention_r{r}",
    )(q_arr, k_arr, k_arr, v_arr, v_arr)
    return o.reshape(m, w), lse.reshape(m, w)


def _dil_merge_kernel(*refs):
    n = len(DIL_CONFIGS)
    o_refs, lse_refs, out_ref = refs[:n], refs[n:2 * n], refs[2 * n]
    lses = [r[...] for r in lse_refs]
    mx = functools.reduce(jnp.maximum, lses)
    ws = [jnp.exp(l - mx) for l in lses]
    num = sum(w * o[...] for w, o in zip(ws, o_refs))
    out_ref[...] = (num / sum(ws)).astype(out_ref.dtype)


def dil_merge(outs, lses):
    m, w = outs[0].shape
    tm = _pick(m, (1024, 512, 256, 128, 8))
    row = pl.BlockSpec((tm, w), lambda i: (i, 0))
    return pl.pallas_call(
        _dil_merge_kernel,
        grid=(m // tm,),
        in_specs=[row] * (2 * len(outs)),
        out_specs=row,
        out_shape=jax.ShapeDtypeStruct((m, w), BF16),
        compiler_params=_params("parallel"),
        name="dil_merge",
    )(*outs, *lses)


def _gated_proj_kernel(osb_ref, od_ref, psb_ref, pd_ref, gsb_ref, gd_ref, o_ref):
    a = jnp.dot(osb_ref[...], psb_ref[...], preferred_element_type=F32)
    b = jnp.dot(od_ref[...], pd_ref[...], preferred_element_type=F32)
    o_ref[...] = (gsb_ref[...].astype(F32) * a + gd_ref[...].astype(F32) * b).astype(o_ref.dtype)


def gated_proj(o_sb, o_d, p_sb, p_d, gates, layer, cast=None):
    m = o_sb.shape[0]
    d = p_sb.shape[-1]
    tm = _pick(m, (1024, 512, 256, 128))
    tn = _pick(d, (1024, 512, 256, 128))
    nd = d // tn
    return _grid_call(
        _gated_proj_kernel, (o_sb, o_d, p_sb, p_d, gates, gates),
        grid=(m // tm, nd),
        in_specs=[pl.BlockSpec((tm, o_sb.shape[1]), lambda i, j: (i, 0)),
                  pl.BlockSpec((tm, o_d.shape[1]), lambda i, j: (i, 0)),
                  pl.BlockSpec((None, p_sb.shape[1], tn), lambda i, j: (layer, 0, j)),
                  pl.BlockSpec((None, p_d.shape[1], tn), lambda i, j: (layer, 0, j)),
                  pl.BlockSpec((tm, tn), lambda i, j: (i, j)),
                  pl.BlockSpec((tm, tn), lambda i, j: (i, j + nd))],
        out_spec=pl.BlockSpec((tm, tn), lambda i, j: (i, j)),
        out_shape=jax.ShapeDtypeStruct((m, d), BF16),
        name="gated_proj", cast=cast)


def _ffn_close_ple_kernel(y_ref, p_ref, h_ref, gd_ref, gu_ref, wp_ref,
                          g_ffn_post_ref, g_gate_ref, g_post_ref, g_next_ref, h_out_ref, xn_out_ref):
    h = h_ref[...] + _rms(y_ref[...], g_ffn_post_ref[...])
    xn = _rms(h, g_gate_ref[...]).astype(BF16)
    low = jnp.dot(xn, gd_ref[...], preferred_element_type=F32)
    gate = jax.nn.sigmoid(jnp.dot(low.astype(BF16), gu_ref[...], preferred_element_type=F32))
    e = jnp.dot(p_ref[...].astype(BF16), wp_ref[...], preferred_element_type=F32) * gate
    h = h + _rms(e, g_post_ref[...])
    h_out_ref[...] = h
    xn_out_ref[...] = _rms(h, g_next_ref[...]).astype(xn_out_ref.dtype)


def ffn_close_ple_block(y, p, h, g_down, g_up, w_ple, g_ffn_post, g_gate, g_post, g_next,
                        layer, next_layer):
    m, d = h.shape
    dp = p.shape[-1]
    tm = _pick(m, (256, 128, 8))
    row = pl.BlockSpec((tm, d), lambda i: (i, 0))
    gain = lambda l: pl.BlockSpec((None, 1, d), lambda i: (l, 0, 0))
    return pl.pallas_call(
        _ffn_close_ple_kernel,
        grid=(m // tm,),
        in_specs=[row,
                  pl.BlockSpec((None, tm, dp), lambda i: (layer, i, 0)),
                  row,
                  pl.BlockSpec((None, d, dp), lambda i: (layer, 0, 0)),
                  pl.BlockSpec((None, dp, d), lambda i: (layer, 0, 0)),
                  pl.BlockSpec((None, dp, d), lambda i: (layer, 0, 0)),
                  gain(layer), gain(layer), gain(layer), gain(next_layer)],
        out_specs=[row, row],
        out_shape=[jax.ShapeDtypeStruct((m, d), F32), jax.ShapeDtypeStruct((m, d), BF16)],
        compiler_params=_params("parallel"),
        name="ffn_close_ple",
    )(y, p, h, g_down, g_up, w_ple, g_ffn_post, g_gate, g_post, g_next)


def kernel(x, p, w_in, w_proj_sb, w_proj_dil, w_out, g_mix_pre, g_mix_post,
           w_ffn_gate, w_ffn_up, w_ffn_down, g_ffn_pre, g_ffn_post,
           w_ple_in, w_ple_gate_down, w_ple_gate_up, g_ple_gate, g_ple_post):
    b, s, d = x.shape
    depth = w_in.shape[0]
    m = b * s
    assert w_in.shape[-1] == W_QKV + 2 * d

    bf = lambda w: w.astype(BF16)
    w_in_bf = bf(w_in[:1])
    w_proj_sb, w_proj_dil = bf(w_proj_sb), bf(w_proj_dil)
    w_ple_in, w_ple_gate_down, w_ple_gate_up = bf(w_ple_in), bf(w_ple_gate_down), bf(w_ple_gate_up)
    gains = lambda g: g.reshape(depth, 1, d)
    g_mix_pre, g_mix_post, g_ffn_pre, g_ffn_post, g_ple_gate, g_ple_post = map(
        gains, (g_mix_pre, g_mix_post, g_ffn_pre, g_ffn_post, g_ple_gate, g_ple_post))
    p = p.reshape(depth, m, p.shape[-1])

    h = x.reshape(m, d)
    xn = rmsnorm_bf16(h, g_mix_pre, 0)
    for i in range(depth):
        qkv = matmul(xn, w_in_bf, 0, col0=0, n_out=W_QKV, out_dtype=BF16, name="in_proj_qkv")
        gates, w_gate_bf = matmul(xn, w_in_bf, 0, col0=W_QKV, n_out=2 * d, out_dtype=BF16,
                                  sigmoid=True, name="in_proj_gates", cast=(w_ffn_gate, i))
        o_sb = sb_attention(qkv, b, s)
        groups = [dil_attention_group(qkv, b, s, g) for g in range(len(DIL_CONFIGS))]
        o_d = dil_merge([o for o, _ in groups], [l for _, l in groups])
        merged, w_out_bf = gated_proj(o_sb, o_d, w_proj_sb, w_proj_dil, gates, i, cast=(w_out, i))
        y, w_up_bf = matmul(merged, w_out_bf, 0, col0=0, n_out=d, out_dtype=F32, name="out_proj",
                            cast=(w_ffn_up, i))
        h, xn = norm_residual(y, h, g_mix_post, g_ffn_pre, i, i)
        hidden, w_down_bf = swiglu_hidden(xn, w_gate_bf, w_up_bf, 0, cast=(w_ffn_down, i))
        if i + 1 < depth:
            y, w_in_bf = matmul(hidden, w_down_bf, 0, col0=0, n_out=d, out_dtype=F32,
                                name="ffn_down", cast=(w_in, i + 1))
        else:
            y = matmul(hidden, w_down_bf, 0, col0=0, n_out=d, out_dtype=F32, name="ffn_down")
        h, xn = ffn_close_ple_block(y, p, h, w_ple_gate_down, w_ple_gate_up, w_ple_in,
                                    g_ffn_post, g_ple_gate, g_ple_post, g_mix_pre, i, (i + 1) % depth)
    return h.reshape(b, s, d)
```

```python
import functools

import jax
import jax.numpy as jnp
from jax import lax
from jax.experimental import pallas as pl
from jax.experimental.pallas import tpu as pltpu

HEAD_DIM = 128
N_HEADS_SB = 8
DIL_CONFIGS = ((128, 1), (512, 4), (2048, 16))
N_HEADS_PER_DIL = 4
EPS = 1e-6

W_SB = N_HEADS_SB * HEAD_DIM
W_DIL_GROUP = N_HEADS_PER_DIL * HEAD_DIM
W_DIL = W_DIL_GROUP * len(DIL_CONFIGS)
W_QKV = 3 * W_SB + 3 * W_DIL
ATT_TILE = DIL_CONFIGS[0][0] // DIL_CONFIGS[0][1]
assert all(w // r == ATT_TILE for w, r in DIL_CONFIGS)
assert ATT_TILE == HEAD_DIM

F32_EXP_UNDERFLOW = 104.0
SB_WINDOW_TILES = 3
SB_Q_TILES_PER_STEP = 4
DIL_TILES_PER_STEP = 2

V7X_VMEM_BYTES = 64 * 1024 * 1024
VMEM_LIMIT_BYTES = V7X_VMEM_BYTES - 6 * 1024 * 1024

BF16 = jnp.bfloat16
F32 = jnp.float32
BF16_SUBLANES = 16


def _params(*sem):
    return pltpu.CompilerParams(dimension_semantics=sem, vmem_limit_bytes=VMEM_LIMIT_BYTES)


def _pick(n, candidates):
    for c in candidates:
        if n % c == 0:
            return c
    raise ValueError(f"no tile in {candidates} divides {n}")


def _matmul_tiles(m, k, n, out_bytes, n_weights=1, col0=0):
    budget = VMEM_LIMIT_BYTES - 8 * 1024 * 1024
    for tm, tn in ((1024, 1280), (1024, 1024), (2048, 512), (1024, 768), (2048, 256), (1024, 512),
                   (1024, 256), (512, 512), (512, 256), (256, 256), (256, 128), (128, 128)):
        if m % tm or n % tn or col0 % tn:
            continue
        need = 2 * (tm * k * 2 + n_weights * k * tn * 2 + tm * tn * out_bytes) + n_weights * tm * tn * 4
        if need <= budget:
            return tm, tn
    raise ValueError(f"no matmul tiling for {(m, k, n)}")


def _rms(x, g):
    return x * lax.rsqrt(jnp.mean(x * x, axis=-1, keepdims=True) + EPS) * g


def _rmsnorm_kernel(x_ref, g_ref, o_ref):
    o_ref[...] = _rms(x_ref[...], g_ref[...]).astype(o_ref.dtype)


def rmsnorm_bf16(x, g, layer):
    m, d = x.shape
    tm = _pick(m, (512, 256, 128, 8))
    return pl.pallas_call(
        _rmsnorm_kernel,
        grid=(m // tm,),
        in_specs=[pl.BlockSpec((tm, d), lambda i: (i, 0)),
                  pl.BlockSpec((None, 1, d), lambda i: (layer, 0, 0))],
        out_specs=pl.BlockSpec((tm, d), lambda i: (i, 0)),
        out_shape=jax.ShapeDtypeStruct((m, d), BF16),
        compiler_params=_params("parallel"),
        name="rmsnorm",
    )(x, g)


def _cast_rows(total_rows, steps):
    for rows in range(BF16_SUBLANES, total_rows + 1, BF16_SUBLANES):
        if total_rows % rows == 0 and total_rows // rows <= steps:
            return rows
    raise ValueError(f"cannot cover {total_rows} rows in {steps} steps")


def _grid_call(body, args, *, grid, in_specs, out_spec, out_shape, name, cast=None):
    if cast is None:
        return pl.pallas_call(
            body, grid=grid, in_specs=in_specs, out_specs=out_spec, out_shape=out_shape,
            compiler_params=_params("parallel", "parallel"), name=name)(*args)
    src, layer = cast
    _, r, c = src.shape
    rows = _cast_rows(r, grid[0] * grid[1])
    blk = lambda i, j: jnp.minimum(i * grid[1] + j, r // rows - 1)
    n_in = len(in_specs)

    def body_with_cast(*refs):
        body(*refs[:n_in], refs[n_in + 1])
        refs[n_in + 2][...] = refs[n_in][...].astype(BF16)

    return pl.pallas_call(
        body_with_cast, grid=grid,
        in_specs=[*in_specs, pl.BlockSpec((None, rows, c), lambda i, j: (layer, blk(i, j), 0))],
        out_specs=[out_spec, pl.BlockSpec((None, rows, c), lambda i, j: (0, blk(i, j), 0))],
        out_shape=[out_shape, jax.ShapeDtypeStruct((1, r, c), BF16)],
        compiler_params=_params("arbitrary", "arbitrary"), name=name)(*args, src)


def _matmul_kernel(x_ref, w_ref, o_ref):
    o_ref[...] = jnp.dot(x_ref[...], w_ref[...], preferred_element_type=F32).astype(o_ref.dtype)


def matmul(x, w, layer, *, col0, n_out, out_dtype, name, cast=None):
    m, k = x.shape
    tm, tn = _matmul_tiles(m, k, n_out, jnp.dtype(out_dtype).itemsize, col0=col0)
    c0 = col0 // tn
    return _grid_call(
        _matmul_kernel, (x, w),
        grid=(m // tm, n_out // tn),
        in_specs=[pl.BlockSpec((tm, k), lambda i, j: (i, 0)),
                  pl.BlockSpec((None, k, tn), lambda i, j: (layer, 0, j + c0))],
        out_spec=pl.BlockSpec((tm, tn), lambda i, j: (i, j)),
        out_shape=jax.ShapeDtypeStruct((m, n_out), out_dtype),
        name=name, cast=cast)


def _swiglu_kernel(x_ref, wg_ref, wu_ref, o_ref):
    x = x_ref[...]
    g = jnp.dot(x, wg_ref[...], preferred_element_type=F32)
    u = jnp.dot(x, wu_ref[...], preferred_element_type=F32)
    o_ref[...] = (g * jax.nn.sigmoid(g) * u).astype(o_ref.dtype)


def swiglu_hidden(x, wg, wu, layer, cast=None):
    m, k = x.shape
    n = wg.shape[-1]
    tm, tn = _matmul_tiles(m, k, n, 2, n_weights=2)
    return _grid_call(
        _swiglu_kernel, (x, wg, wu),
        grid=(m // tm, n // tn),
        in_specs=[pl.BlockSpec((tm, k), lambda i, j: (i, 0)),
                  pl.BlockSpec((None, k, tn), lambda i, j: (layer, 0, j)),
                  pl.BlockSpec((None, k, tn), lambda i, j: (layer, 0, j))],
        out_spec=pl.BlockSpec((tm, tn), lambda i, j: (i, j)),
        out_shape=jax.ShapeDtypeStruct((m, n), BF16),
        name="swiglu_hidden", cast=cast)


def _norm_residual_kernel(y_ref, h_ref, gpost_ref, gnext_ref, h_out_ref, xn_out_ref):
    h = h_ref[...] + _rms(y_ref[...], gpost_ref[...])
    h_out_ref[...] = h
    xn_out_ref[...] = _rms(h, gnext_ref[...]).astype(xn_out_ref.dtype)


def norm_residual(y, h, g_post, g_next, layer, next_layer):
    m, d = h.shape
    tm = _pick(m, (256, 128, 8))
    row = pl.BlockSpec((tm, d), lambda i: (i, 0))
    return pl.pallas_call(
        _norm_residual_kernel,
        grid=(m // tm,),
        in_specs=[row, row,
                  pl.BlockSpec((None, 1, d), lambda i: (layer, 0, 0)),
                  pl.BlockSpec((None, 1, d), lambda i: (next_layer, 0, 0))],
        out_specs=[row, row],
        out_shape=[jax.ShapeDtypeStruct((m, d), F32), jax.ShapeDtypeStruct((m, d), BF16)],
        compiler_params=_params("parallel"),
        name="norm_residual",
    )(y, h, g_post, g_next)


def _sb_scores(q, k, q0, k0):
    z = lax.dot_general(q, k, (((1,), (1,)), ((), ())), preferred_element_type=F32) * HEAD_DIM ** -0.5
    row = lax.broadcasted_iota(jnp.int32, z.shape, 0)
    col = lax.broadcasted_iota(jnp.int32, z.shape, 1)
    causal = (col + k0) < (row + q0)
    log_beta = jnp.minimum(z, 0.0) - jnp.log(1.0 + jnp.exp(-jnp.abs(z)))
    log_stay = jnp.where(causal, log_beta - z, 0.0)
    return log_beta, log_stay, causal


def _suffix_sums(tiles, tri2):
    t = ATT_TILE
    x = jnp.concatenate(tiles, axis=0) if len(tiles) > 1 else tiles[0]
    hi = x.astype(BF16)
    lo = (x - hi.astype(F32)).astype(BF16)
    sums = jnp.dot(jnp.concatenate([hi, lo], axis=1), tri2, preferred_element_type=F32)
    return [(sums[n * t:(n + 1) * t, :t], sums[n * t:(n + 1) * t, t:]) for n in range(len(tiles))]


def _sb_attention_kernel(q_ref, k_ref, v_ref, tri2_ref, o_ref):
    s_len = q_ref.shape[0]
    t = ATT_TILE
    nw = SB_WINDOW_TILES
    wk = nw * t

    def windows(tiles):
        pre = []
        for i in tiles:
            q0 = pl.multiple_of(i * t, t)
            first = jnp.maximum(i - (nw - 1), 0)
            k0 = pl.multiple_of(first * t, t)
            q = q_ref[pl.ds(q0, t), :]
            pre.append((q, q0, first, k0) + _sb_scores(q, k_ref[pl.ds(k0, wk), :], q0, k0))
        sums = _suffix_sums([p[5][:, d * t:(d + 1) * t] for p in pre for d in range(nw)], tri2_ref[...])
        out = []
        for u, (q, q0, first, k0, log_beta, _, causal) in enumerate(pre):
            after, run = [], jnp.zeros((t, t), F32)
            for d in reversed(range(nw)):
                cum, tot = sums[u * nw + d]
                after.append(cum + run)
                run = run + tot
            after = jnp.concatenate(after[::-1], axis=1)
            w = jnp.where(causal, jnp.exp(log_beta + after), 0.0)
            acc = jnp.dot(w.astype(BF16), v_ref[pl.ds(k0, wk), :], preferred_element_type=F32)
            out.append((q, q0, first - 1, run, acc))
        return out

    def more(j, run):
        return jnp.logical_and(j >= 0, jnp.max(run) > -F32_EXP_UNDERFLOW)

    def tail(q, q0, j, run, acc):
        def k_tile(state):
            j, _, run, acc = state
            k0 = pl.multiple_of(j * t, t)
            log_beta, log_stay, _ = _sb_scores(q, k_ref[pl.ds(k0, t), :], q0, k0)
            (cum, tot), = _suffix_sums([log_stay], tri2_ref[...])
            w = jnp.exp(log_beta + run + cum)
            acc = acc + jnp.dot(w.astype(BF16), v_ref[pl.ds(k0, t), :], preferred_element_type=F32)
            run = run + tot
            return j - 1, more(j - 1, run), run, acc

        return lax.while_loop(lambda st: st[1], k_tile, (j, more(j, run), run, acc))[3]

    def q_group(g, carry):
        wins = windows([g * SB_Q_TILES_PER_STEP + u for u in range(SB_Q_TILES_PER_STEP)])
        for q, q0, j, run, acc in wins:
            o_ref[pl.ds(q0, t), :] = tail(q, q0, j, run, acc).astype(o_ref.dtype)
        return carry

    lax.fori_loop(0, s_len // (t * SB_Q_TILES_PER_STEP), q_group, 0)


def _suffix_sum_matrix():
    r = jnp.arange(ATT_TILE)
    strict_lower = (r[:, None] > r[None, :]).astype(BF16)
    half = jnp.concatenate([strict_lower, jnp.ones((ATT_TILE, ATT_TILE), BF16)], axis=1)
    return jnp.concatenate([half, half], axis=0)


def sb_attention(qkv, batch, seq):
    m = qkv.shape[0]
    h = N_HEADS_SB
    assert seq % (ATT_TILE * SB_Q_TILES_PER_STEP) == 0 and seq >= SB_WINDOW_TILES * ATT_TILE
    blk = lambda off: pl.BlockSpec((seq, HEAD_DIM), lambda b, n: (b, n + off))
    tri2 = _suffix_sum_matrix()
    return pl.pallas_call(
        _sb_attention_kernel,
        grid=(batch, h),
        in_specs=[blk(0), blk(h), blk(2 * h), pl.BlockSpec(tri2.shape, lambda b, n: (0, 0))],
        out_specs=blk(0),
        out_shape=jax.ShapeDtypeStruct((m, W_SB), BF16),
        compiler_params=_params("parallel", "parallel"),
        name="sb_attention",
    )(qkv, qkv, qkv, tri2)


def _dil_attention_kernel(q_ref, kp_ref, kc_ref, vp_ref, vc_ref, o_ref, lse_ref, *, dilation):
    t = ATT_TILE
    scale = HEAD_DIM ** -0.5
    i = pl.program_id(2)
    row = lax.broadcasted_iota(jnp.int32, (t, 2 * t), 0)
    col = lax.broadcasted_iota(jnp.int32, (t, 2 * t), 1)
    steps = t + row - col
    in_band = jnp.logical_and(steps >= 0, steps <= t)
    dist = (steps * dilation).astype(F32)
    lane = lax.broadcasted_iota(jnp.int32, (t, HEAD_DIM), 1)
    nt = (((1,), (1,)), ((), ()))
    for a in range(q_ref.shape[0] // t):
        rows = slice(a * t, (a + 1) * t)
        prev = slice((a - 1) * t, a * t)
        valid = in_band if a else jnp.logical_and(in_band, jnp.logical_or(col >= t, i > 0))
        lse = jnp.zeros((t, HEAD_DIM), F32)
        for hh in range(N_HEADS_PER_DIL):
            slope = 2.0 ** (-8.0 * (hh + 1) / N_HEADS_PER_DIL)
            sl = slice(hh * HEAD_DIM, (hh + 1) * HEAD_DIM)
            k = jnp.concatenate([kc_ref[prev, sl] if a else kp_ref[:, sl], kc_ref[rows, sl]], axis=0)
            v = jnp.concatenate([vc_ref[prev, sl] if a else vp_ref[:, sl], vc_ref[rows, sl]], axis=0)
            s = lax.dot_general(q_ref[rows, sl], k, nt, preferred_element_type=F32) * scale
            s = jnp.where(valid, s - slope * dist, -jnp.inf)
            mx = jnp.max(s, axis=1, keepdims=True)
            e = jnp.exp(s - mx)
            den = jnp.sum(e, axis=1, keepdims=True)
            pv = jnp.dot(e.astype(BF16), v, preferred_element_type=F32)
            o_ref[rows, sl] = (pv / den).astype(o_ref.dtype)
            lse = jnp.where(lane == hh, mx + jnp.log(den), lse)
        lse_ref[rows, :] = lse


def dil_attention_group(qkv, batch, seq, group):
    m, wq = qkv.shape
    _, r = DIL_CONFIGS[group]
    length = seq // r
    t = ATT_TILE
    nb = length // t
    w = W_DIL_GROUP
    assert wq % w == 0 and length % t == 0
    q_col = (3 * W_SB) // w + group
    k_col = q_col + W_DIL // w
    v_col = k_col + W_DIL // w
    if r == 1:
        q_arr = k_arr = v_arr = qkv
        cols = wq // w
    else:
        take = lambda col: qkv[:, col * w:(col + 1) * w].reshape(m // r, r * w)
        q_arr, k_arr, v_arr = take(q_col), take(k_col), take(v_col)
        q_col = k_col = v_col = 0
        cols = 1

    tiles = _pick(nb, (DIL_TILES_PER_STEP, 1))
    ns = nb // tiles

    def spec(col, prev):
        if prev:
            return pl.BlockSpec((t, w), lambda b, c, i: (b * nb + jnp.maximum(i * tiles - 1, 0), c * cols + col))
        return pl.BlockSpec((tiles * t, w), lambda b, c, i: (b * ns + i, c * cols + col))

    out_spec = lambda width: pl.BlockSpec((tiles * t, width), lambda b, c, i: (b * ns + i, c))
    o, lse = pl.pallas_call(
        functools.partial(_dil_attention_kernel, dilation=r),
        grid=(batch, r, ns),
        in_specs=[spec(q_col, False), spec(k_col, True), spec(k_col, False),
                  spec(v_col, True), spec(v_col, False)],
        out_specs=[out_spec(w), out_spec(HEAD_DIM)],
        out_shape=[jax.ShapeDtypeStruct((m // r, r * w), BF16),
                   jax.ShapeDtypeStruct((m // r, r * HEAD_DIM), F32)],
        compiler_params=_params("parallel", "parallel", "arbitrary"),
        name=f"dil_attention_r{r}",
    )(q_arr, k_arr, k_arr, v_arr, v_arr)
    return o.reshape(m, w), lse.reshape(m, HEAD_DIM)


def _dil_merge_kernel(*refs):
    n = len(DIL_CONFIGS)
    o_refs, lse_refs, out_ref = refs[:n], refs[n:2 * n], refs[2 * n]
    lses = [r[...] for r in lse_refs]
    mx = functools.reduce(jnp.maximum, lses)
    ws = [jnp.exp(l - mx) for l in lses]
    total = sum(ws)
    alphas = [w / total for w in ws]
    for hh in range(N_HEADS_PER_DIL):
        sl = slice(hh * HEAD_DIM, (hh + 1) * HEAD_DIM)
        merged = sum(a[:, hh:hh + 1] * o[:, sl] for a, o in zip(alphas, o_refs))
        out_ref[:, sl] = merged.astype(out_ref.dtype)


def dil_merge(outs, lses):
    m, w = outs[0].shape
    tm = _pick(m, (1024, 512, 256, 128, 8))
    row = pl.BlockSpec((tm, w), lambda i: (i, 0))
    lse_row = pl.BlockSpec((tm, HEAD_DIM), lambda i: (i, 0))
    return pl.pallas_call(
        _dil_merge_kernel,
        grid=(m // tm,),
        in_specs=[row] * len(outs) + [lse_row] * len(lses),
        out_specs=row,
        out_shape=jax.ShapeDtypeStruct((m, w), BF16),
        compiler_params=_params("parallel"),
        name="dil_merge",
    )(*outs, *lses)


def _gated_proj_kernel(osb_ref, od_ref, psb_ref, pd_ref, gsb_ref, gd_ref, o_ref):
    a = jnp.dot(osb_ref[...], psb_ref[...], preferred_element_type=F32)
    b = jnp.dot(od_ref[...], pd_ref[...], preferred_element_type=F32)
    gate_sb = jax.nn.sigmoid(gsb_ref[...].astype(F32))
    gate_d = jax.nn.sigmoid(gd_ref[...].astype(F32))
    o_ref[...] = (gate_sb * a + gate_d * b).astype(o_ref.dtype)


def gated_proj(o_sb, o_d, p_sb, p_d, gates, layer, cast=None):
    m = o_sb.shape[0]
    d = p_sb.shape[-1]
    tm = _pick(m, (1024, 512, 256, 128))
    tn = _pick(d, (1024, 512, 256, 128))
    nd = d // tn
    return _grid_call(
        _gated_proj_kernel, (o_sb, o_d, p_sb, p_d, gates, gates),
        grid=(m // tm, nd),
        in_specs=[pl.BlockSpec((tm, o_sb.shape[1]), lambda i, j: (i, 0)),
                  pl.BlockSpec((tm, o_d.shape[1]), lambda i, j: (i, 0)),
                  pl.BlockSpec((None, p_sb.shape[1], tn), lambda i, j: (layer, 0, j)),
                  pl.BlockSpec((None, p_d.shape[1], tn), lambda i, j: (layer, 0, j)),
                  pl.BlockSpec((tm, tn), lambda i, j: (i, j)),
                  pl.BlockSpec((tm, tn), lambda i, j: (i, j + nd))],
        out_spec=pl.BlockSpec((tm, tn), lambda i, j: (i, j)),
        out_shape=jax.ShapeDtypeStruct((m, d), BF16),
        name="gated_proj", cast=cast)


def _ffn_close_ple_kernel(y_ref, p_ref, h_ref, gd_ref, gu_ref, wp_ref,
                          g_ffn_post_ref, g_gate_ref, g_post_ref, g_next_ref, h_out_ref, xn_out_ref):
    h = h_ref[...] + _rms(y_ref[...], g_ffn_post_ref[...])
    xn = _rms(h, g_gate_ref[...]).astype(BF16)
    low = jnp.dot(xn, gd_ref[...], preferred_element_type=F32)
    gate = jax.nn.sigmoid(jnp.dot(low.astype(BF16), gu_ref[...], preferred_element_type=F32))
    e = jnp.dot(p_ref[...].astype(BF16), wp_ref[...], preferred_element_type=F32) * gate
    h = h + _rms(e, g_post_ref[...])
    h_out_ref[...] = h
    xn_out_ref[...] = _rms(h, g_next_ref[...]).astype(xn_out_ref.dtype)


def ffn_close_ple_block(y, p, h, g_down, g_up, w_ple, g_ffn_post, g_gate, g_post, g_next,
                        layer, next_layer):
    m, d = h.shape
    dp = p.shape[-1]
    tm = _pick(m, (256, 128, 8))
    row = pl.BlockSpec((tm, d), lambda i: (i, 0))
    gain = lambda l: pl.BlockSpec((None, 1, d), lambda i: (l, 0, 0))
    return pl.pallas_call(
        _ffn_close_ple_kernel,
        grid=(m // tm,),
        in_specs=[row,
                  pl.BlockSpec((None, tm, dp), lambda i: (layer, i, 0)),
                  row,
                  pl.BlockSpec((None, d, dp), lambda i: (layer, 0, 0)),
                  pl.BlockSpec((None, dp, d), lambda i: (layer, 0, 0)),
                  pl.BlockSpec((None, dp, d), lambda i: (layer, 0, 0)),
                  gain(layer), gain(layer), gain(layer), gain(next_layer)],
        out_specs=[row, row],
        out_shape=[jax.ShapeDtypeStruct((m, d), F32), jax.ShapeDtypeStruct((m, d), BF16)],
        compiler_params=_params("parallel"),
        name="ffn_close_ple",
    )(y, p, h, g_down, g_up, w_ple, g_ffn_post, g_gate, g_post, g_next)


def kernel(x, p, w_in, w_proj_sb, w_proj_dil, w_out, g_mix_pre, g_mix_post,
           w_ffn_gate, w_ffn_up, w_ffn_down, g_ffn_pre, g_ffn_post,
           w_ple_in, w_ple_gate_down, w_ple_gate_up, g_ple_gate, g_ple_post):
    b, s, d = x.shape
    depth = w_in.shape[0]
    m = b * s
    assert w_in.shape[-1] == W_QKV + 2 * d

    bf = lambda w: w.astype(BF16)
    w_in_bf = bf(w_in[:1])
    w_proj_sb, w_proj_dil = bf(w_proj_sb), bf(w_proj_dil)
    w_ple_in, w_ple_gate_down, w_ple_gate_up = bf(w_ple_in), bf(w_ple_gate_down), bf(w_ple_gate_up)
    gains = lambda g: g.reshape(depth, 1, d)
    g_mix_pre, g_mix_post, g_ffn_pre, g_ffn_post, g_ple_gate, g_ple_post = map(
        gains, (g_mix_pre, g_mix_post, g_ffn_pre, g_ffn_post, g_ple_gate, g_ple_post))
    p = p.reshape(depth, m, p.shape[-1])

    h = x.reshape(m, d)
    xn = rmsnorm_bf16(h, g_mix_pre, 0)
    for i in range(depth):
        qkv = matmul(xn, w_in_bf, 0, col0=0, n_out=W_QKV, out_dtype=BF16, name="in_proj_qkv")
        gates, w_gate_bf = matmul(xn, w_in_bf, 0, col0=W_QKV, n_out=2 * d, out_dtype=BF16,
                                  name="in_proj_gates", cast=(w_ffn_gate, i))
        o_sb = sb_attention(qkv, b, s)
        groups = [dil_attention_group(qkv, b, s, g) for g in range(len(DIL_CONFIGS))]
        o_d = dil_merge([o for o, _ in groups], [l for _, l in groups])
        merged, w_out_bf = gated_proj(o_sb, o_d, w_proj_sb, w_proj_dil, gates, i, cast=(w_out, i))
        y, w_up_bf = matmul(merged, w_out_bf, 0, col0=0, n_out=d, out_dtype=F32, name="out_proj",
                            cast=(w_ffn_up, i))
        h, xn = norm_residual(y, h, g_mix_post, g_ffn_pre, i, i)
        hidden, w_down_bf = swiglu_hidden(xn, w_gate_bf, w_up_bf, 0, cast=(w_ffn_down, i))
        if i + 1 < depth:
            y, w_in_bf = matmul(hidden, w_down_bf, 0, col0=0, n_out=d, out_dtype=F32,
                                name="ffn_down", cast=(w_in, i + 1))
        else:
            y = matmul(hidden, w_down_bf, 0, col0=0, n_out=d, out_dtype=F32, name="ffn_down")
        h, xn = ffn_close_ple_block(y, p, h, w_ple_gate_down, w_ple_gate_up, w_ple_in,
                                    g_ffn_post, g_ple_gate, g_ple_post, g_mix_pre, i, (i + 1) % depth)
    return h.reshape(b, s, d)
```

```python
import functools

import jax
import jax.numpy as jnp
from jax import lax
from jax.experimental import pallas as pl
from jax.experimental.pallas import tpu as pltpu

HEAD_DIM = 128
N_HEADS_SB = 8
DIL_CONFIGS = ((128, 1), (512, 4), (2048, 16))
N_HEADS_PER_DIL = 4
EPS = 1e-6

W_SB = N_HEADS_SB * HEAD_DIM
W_DIL_GROUP = N_HEADS_PER_DIL * HEAD_DIM
W_DIL = W_DIL_GROUP * len(DIL_CONFIGS)
W_QKV = 3 * W_SB + 3 * W_DIL
ATT_TILE = DIL_CONFIGS[0][0] // DIL_CONFIGS[0][1]
assert all(w // r == ATT_TILE for w, r in DIL_CONFIGS)
assert ATT_TILE == HEAD_DIM

F32_EXP_UNDERFLOW = 104.0
SB_WINDOW_TILES = 3
SB_Q_TILES_PER_STEP = 4
DIL_TILES_PER_STEP = 2

V7X_VMEM_BYTES = 64 * 1024 * 1024
VMEM_LIMIT_BYTES = V7X_VMEM_BYTES - 6 * 1024 * 1024

BF16 = jnp.bfloat16
F32 = jnp.float32
BF16_SUBLANES = 16


def _params(*sem):
    return pltpu.CompilerParams(dimension_semantics=sem, vmem_limit_bytes=VMEM_LIMIT_BYTES)


def _pick(n, candidates):
    for c in candidates:
        if n % c == 0:
            return c
    raise ValueError(f"no tile in {candidates} divides {n}")


def _matmul_tiles(m, k, n, out_bytes, n_weights=1, col0=0):
    budget = VMEM_LIMIT_BYTES - 8 * 1024 * 1024
    for tm, tn in ((1024, 1280), (1024, 1024), (2048, 512), (1024, 768), (2048, 256), (1024, 512),
                   (1024, 256), (512, 512), (512, 256), (256, 256), (256, 128), (128, 128)):
        if m % tm or n % tn or col0 % tn:
            continue
        need = 2 * (tm * k * 2 + n_weights * k * tn * 2 + tm * tn * out_bytes) + n_weights * tm * tn * 4
        if need <= budget:
            return tm, tn
    raise ValueError(f"no matmul tiling for {(m, k, n)}")


def _rms(x, g):
    return x * lax.rsqrt(jnp.mean(x * x, axis=-1, keepdims=True) + EPS) * g


def _rmsnorm_kernel(x_ref, g_ref, o_ref):
    o_ref[...] = _rms(x_ref[...], g_ref[...]).astype(o_ref.dtype)


def rmsnorm_bf16(x, g, layer):
    m, d = x.shape
    tm = _pick(m, (512, 256, 128, 8))
    return pl.pallas_call(
        _rmsnorm_kernel,
        grid=(m // tm,),
        in_specs=[pl.BlockSpec((tm, d), lambda i: (i, 0)),
                  pl.BlockSpec((None, 1, d), lambda i: (layer, 0, 0))],
        out_specs=pl.BlockSpec((tm, d), lambda i: (i, 0)),
        out_shape=jax.ShapeDtypeStruct((m, d), BF16),
        compiler_params=_params("parallel"),
        name="rmsnorm",
    )(x, g)


def _cast_rows(total_rows, steps):
    for rows in range(BF16_SUBLANES, total_rows + 1, BF16_SUBLANES):
        if total_rows % rows == 0 and total_rows // rows <= steps:
            return rows
    raise ValueError(f"cannot cover {total_rows} rows in {steps} steps")


def _grid_call(body, args, *, grid, in_specs, out_spec, out_shape, name, cast=None):
    if cast is None:
        return pl.pallas_call(
            body, grid=grid, in_specs=in_specs, out_specs=out_spec, out_shape=out_shape,
            compiler_params=_params("parallel", "parallel"), name=name)(*args)
    src, layer = cast
    _, r, c = src.shape
    rows = _cast_rows(r, grid[0] * grid[1])
    blk = lambda i, j: jnp.minimum(i * grid[1] + j, r // rows - 1)
    n_in = len(in_specs)

    def body_with_cast(*refs):
        body(*refs[:n_in], refs[n_in + 1])
        refs[n_in + 2][...] = refs[n_in][...].astype(BF16)

    return pl.pallas_call(
        body_with_cast, grid=grid,
        in_specs=[*in_specs, pl.BlockSpec((None, rows, c), lambda i, j: (layer, blk(i, j), 0))],
        out_specs=[out_spec, pl.BlockSpec((None, rows, c), lambda i, j: (0, blk(i, j), 0))],
        out_shape=[out_shape, jax.ShapeDtypeStruct((1, r, c), BF16)],
        compiler_params=_params("arbitrary", "arbitrary"), name=name)(*args, src)


def _matmul_kernel(x_ref, w_ref, o_ref):
    o_ref[...] = jnp.dot(x_ref[...], w_ref[...], preferred_element_type=F32).astype(o_ref.dtype)


def matmul(x, w, layer, *, col0, n_out, out_dtype, name, cast=None):
    m, k = x.shape
    tm, tn = _matmul_tiles(m, k, n_out, jnp.dtype(out_dtype).itemsize, col0=col0)
    c0 = col0 // tn
    return _grid_call(
        _matmul_kernel, (x, w),
        grid=(m // tm, n_out // tn),
        in_specs=[pl.BlockSpec((tm, k), lambda i, j: (i, 0)),
                  pl.BlockSpec((None, k, tn), lambda i, j: (layer, 0, j + c0))],
        out_spec=pl.BlockSpec((tm, tn), lambda i, j: (i, j)),
        out_shape=jax.ShapeDtypeStruct((m, n_out), out_dtype),
        name=name, cast=cast)


def _swiglu_kernel(x_ref, wg_ref, wu_ref, o_ref):
    x = x_ref[...]
    g = jnp.dot(x, wg_ref[...], preferred_element_type=F32)
    u = jnp.dot(x, wu_ref[...], preferred_element_type=F32)
    o_ref[...] = (g * jax.nn.sigmoid(g) * u).astype(o_ref.dtype)


def swiglu_hidden(x, wg, wu, layer, cast=None):
    m, k = x.shape
    n = wg.shape[-1]
    tm, tn = _matmul_tiles(m, k, n, 2, n_weights=2)
    return _grid_call(
        _swiglu_kernel, (x, wg, wu),
        grid=(m // tm, n // tn),
        in_specs=[pl.BlockSpec((tm, k), lambda i, j: (i, 0)),
                  pl.BlockSpec((None, k, tn), lambda i, j: (layer, 0, j)),
                  pl.BlockSpec((None, k, tn), lambda i, j: (layer, 0, j))],
        out_spec=pl.BlockSpec((tm, tn), lambda i, j: (i, j)),
        out_shape=jax.ShapeDtypeStruct((m, n), BF16),
        name="swiglu_hidden", cast=cast)


def _norm_residual_kernel(y_ref, h_ref, gpost_ref, gnext_ref, h_out_ref, xn_out_ref):
    h = h_ref[...] + _rms(y_ref[...], gpost_ref[...])
    h_out_ref[...] = h
    xn_out_ref[...] = _rms(h, gnext_ref[...]).astype(xn_out_ref.dtype)


def norm_residual(y, h, g_post, g_next, layer, next_layer):
    m, d = h.shape
    tm = _pick(m, (256, 128, 8))
    row = pl.BlockSpec((tm, d), lambda i: (i, 0))
    return pl.pallas_call(
        _norm_residual_kernel,
        grid=(m // tm,),
        in_specs=[row, row,
                  pl.BlockSpec((None, 1, d), lambda i: (layer, 0, 0)),
                  pl.BlockSpec((None, 1, d), lambda i: (next_layer, 0, 0))],
        out_specs=[row, row],
        out_shape=[jax.ShapeDtypeStruct((m, d), F32), jax.ShapeDtypeStruct((m, d), BF16)],
        compiler_params=_params("parallel"),
        name="norm_residual",
    )(y, h, g_post, g_next)


def _sb_scores(q, k, q0, k0):
    z = lax.dot_general(q, k, (((1,), (1,)), ((), ())), preferred_element_type=F32) * HEAD_DIM ** -0.5
    row = lax.broadcasted_iota(jnp.int32, z.shape, 0)
    col = lax.broadcasted_iota(jnp.int32, z.shape, 1)
    causal = (col + k0) < (row + q0)
    log_beta = jnp.minimum(z, 0.0) - jnp.log(1.0 + jnp.exp(-jnp.abs(z)))
    log_stay = jnp.where(causal, log_beta - z, 0.0)
    return log_beta, log_stay, causal


def _suffix_sums(tiles, tri2):
    t = ATT_TILE
    x = jnp.concatenate(tiles, axis=0) if len(tiles) > 1 else tiles[0]
    hi = x.astype(BF16)
    lo = (x - hi.astype(F32)).astype(BF16)
    sums = jnp.dot(jnp.concatenate([hi, lo], axis=1), tri2, preferred_element_type=F32)
    return [(sums[n * t:(n + 1) * t, :t], sums[n * t:(n + 1) * t, t:]) for n in range(len(tiles))]


def _sb_attention_kernel(q_ref, k_ref, v_ref, tri2_ref, o_ref):
    s_len = q_ref.shape[0]
    t = ATT_TILE
    nw = SB_WINDOW_TILES
    wk = nw * t

    def windows(tiles):
        pre = []
        for i in tiles:
            q0 = pl.multiple_of(i * t, t)
            first = jnp.maximum(i - (nw - 1), 0)
            k0 = pl.multiple_of(first * t, t)
            q = q_ref[pl.ds(q0, t), :]
            pre.append((q, q0, first, k0) + _sb_scores(q, k_ref[pl.ds(k0, wk), :], q0, k0))
        sums = _suffix_sums([p[5][:, d * t:(d + 1) * t] for p in pre for d in range(nw)], tri2_ref[...])
        out = []
        for u, (q, q0, first, k0, log_beta, _, causal) in enumerate(pre):
            after, run = [], jnp.zeros((t, t), F32)
            for d in reversed(range(nw)):
                cum, tot = sums[u * nw + d]
                after.append(cum + run)
                run = run + tot
            after = jnp.concatenate(after[::-1], axis=1)
            w = jnp.where(causal, jnp.exp(log_beta + after), 0.0)
            acc = jnp.dot(w.astype(BF16), v_ref[pl.ds(k0, wk), :], preferred_element_type=F32)
            out.append((q, q0, first - 1, run, acc))
        return out

    def more(j, run):
        return jnp.logical_and(j >= 0, jnp.max(run) > -F32_EXP_UNDERFLOW)

    def tail(q, q0, j, run, acc):
        def k_tile(state):
            j, _, run, acc = state
            k0 = pl.multiple_of(j * t, t)
            log_beta, log_stay, _ = _sb_scores(q, k_ref[pl.ds(k0, t), :], q0, k0)
            (cum, tot), = _suffix_sums([log_stay], tri2_ref[...])
            w = jnp.exp(log_beta + run + cum)
            acc = acc + jnp.dot(w.astype(BF16), v_ref[pl.ds(k0, t), :], preferred_element_type=F32)
            run = run + tot
            return j - 1, more(j - 1, run), run, acc

        return lax.while_loop(lambda st: st[1], k_tile, (j, more(j, run), run, acc))[3]

    def q_group(g, carry):
        wins = windows([g * SB_Q_TILES_PER_STEP + u for u in range(SB_Q_TILES_PER_STEP)])
        for q, q0, j, run, acc in wins:
            o_ref[pl.ds(q0, t), :] = tail(q, q0, j, run, acc).astype(o_ref.dtype)
        return carry

    lax.fori_loop(0, s_len // (t * SB_Q_TILES_PER_STEP), q_group, 0)


def _suffix_sum_matrix():
    r = jnp.arange(ATT_TILE)
    strict_lower = (r[:, None] > r[None, :]).astype(BF16)
    half = jnp.concatenate([strict_lower, jnp.ones((ATT_TILE, ATT_TILE), BF16)], axis=1)
    return jnp.concatenate([half, half], axis=0)


def sb_attention(qkv, batch, seq):
    m = qkv.shape[0]
    h = N_HEADS_SB
    assert seq % (ATT_TILE * SB_Q_TILES_PER_STEP) == 0 and seq >= SB_WINDOW_TILES * ATT_TILE
    blk = lambda off: pl.BlockSpec((seq, HEAD_DIM), lambda b, n: (b, n + off))
    tri2 = _suffix_sum_matrix()
    return pl.pallas_call(
        _sb_attention_kernel,
        grid=(batch, h),
        in_specs=[blk(0), blk(h), blk(2 * h), pl.BlockSpec(tri2.shape, lambda b, n: (0, 0))],
        out_specs=blk(0),
        out_shape=jax.ShapeDtypeStruct((m, W_SB), BF16),
        compiler_params=_params("parallel", "parallel"),
        name="sb_attention",
    )(qkv, qkv, qkv, tri2)


def _dil_attention_kernel(q_ref, kp_ref, kc_ref, vp_ref, vc_ref, o_ref, lse_ref):
    t = ATT_TILE
    i = pl.program_id(1)
    in_band, own_tile, dist = _dil_band(1)
    lane = lax.broadcasted_iota(jnp.int32, (t, HEAD_DIM), 1)
    for a in range(q_ref.shape[0] // t):
        rows = slice(a * t, (a + 1) * t)
        prev = slice((a - 1) * t, a * t)
        valid = in_band if a else jnp.logical_and(in_band, jnp.logical_or(own_tile, i > 0))
        lse = jnp.zeros((t, HEAD_DIM), F32)
        for hh in range(N_HEADS_PER_DIL):
            sl = slice(hh * HEAD_DIM, (hh + 1) * HEAD_DIM)
            k = jnp.concatenate([kc_ref[prev, sl] if a else kp_ref[:, sl], kc_ref[rows, sl]], axis=0)
            v = jnp.concatenate([vc_ref[prev, sl] if a else vp_ref[:, sl], vc_ref[rows, sl]], axis=0)
            out, lse_h = _dil_tile(q_ref[rows, sl], k, v, valid, dist, hh)
            o_ref[rows, sl] = out.astype(o_ref.dtype)
            lse = jnp.where(lane == hh, lse_h, lse)
        lse_ref[rows, :] = lse


def _dil_tile(q, k, v, valid, dist, head):
    slope = 2.0 ** (-8.0 * (head + 1) / N_HEADS_PER_DIL)
    s = lax.dot_general(q, k, (((1,), (1,)), ((), ())), preferred_element_type=F32) * HEAD_DIM ** -0.5
    s = jnp.where(valid, s - slope * dist, -jnp.inf)
    mx = jnp.max(s, axis=1, keepdims=True)
    e = jnp.exp(s - mx)
    den = jnp.sum(e, axis=1, keepdims=True)
    pv = jnp.dot(e.astype(BF16), v, preferred_element_type=F32)
    return pv / den, mx + jnp.log(den)


def _dil_band(dilation):
    t = ATT_TILE
    row = lax.broadcasted_iota(jnp.int32, (t, 2 * t), 0)
    col = lax.broadcasted_iota(jnp.int32, (t, 2 * t), 1)
    steps = t + row - col
    return jnp.logical_and(steps >= 0, steps <= t), col >= t, (steps * dilation).astype(F32)


def _dil_strided_kernel(q_ref, kp_ref, kc_ref, vp_ref, vc_ref, o_ref, lse_ref,
                        qw, kpw, kcw, vpw, vcw, ow, lse_rows, *, dilation):
    t = ATT_TILE
    r = dilation
    half = r // 2
    heads = N_HEADS_PER_DIL
    i = pl.program_id(1)
    for src, words in ((q_ref, qw), (kp_ref, kpw), (kc_ref, kcw), (vp_ref, vpw), (vc_ref, vcw)):
        for hh in range(heads):
            words[hh] = pltpu.bitcast(src[:, hh * HEAD_DIM:(hh + 1) * HEAD_DIM], jnp.uint32)
    in_band, own_tile, dist = _dil_band(r)
    valid = jnp.logical_and(in_band, jnp.logical_or(own_tile, i > 0))
    lane = lax.broadcasted_iota(jnp.int32, (t, HEAD_DIM), 1)
    high = jnp.uint32(0xFFFF0000)

    def unpack(words, odd):
        return pltpu.bitcast(words & high if odd else words << 16, F32).astype(BF16)

    def bf16_bits(x):
        return pltpu.bitcast(x.astype(BF16).astype(F32), jnp.uint32)

    def class_pair(p, carry):
        sel = pl.ds(p, t, stride=half)
        lses = [jnp.zeros((t, HEAD_DIM), F32)] * 2
        for hh in range(heads):
            wq, wkp, wkc, wvp, wvc = (w[hh, sel, :] for w in (qw, kpw, kcw, vpw, vcw))
            outs = []
            for odd in (0, 1):
                k = jnp.concatenate([unpack(wkp, odd), unpack(wkc, odd)], axis=0)
                v = jnp.concatenate([unpack(wvp, odd), unpack(wvc, odd)], axis=0)
                out, lse_h = _dil_tile(unpack(wq, odd), k, v, valid, dist, hh)
                outs.append(out)
                lses[odd] = jnp.where(lane == hh, lse_h, lses[odd])
            ow[hh, sel, :] = (bf16_bits(outs[0]) >> 16) | (bf16_bits(outs[1]) & high)
        for odd in (0, 1):
            lse_rows[pl.ds(2 * p + odd, t, stride=r), :] = lses[odd]
        return carry

    lax.fori_loop(0, half, class_pair, 0)
    for hh in range(heads):
        o_ref[:, hh * HEAD_DIM:(hh + 1) * HEAD_DIM] = pltpu.bitcast(ow[hh], BF16)
    lse_ref[...] = lse_rows[...]


def dil_attention_group(qkv, batch, seq, group):
    m, wq = qkv.shape
    _, r = DIL_CONFIGS[group]
    t = ATT_TILE
    w = W_DIL_GROUP
    assert wq % w == 0 and seq % (t * r) == 0
    q_col = (3 * W_SB) // w + group
    k_col = q_col + W_DIL // w
    v_col = k_col + W_DIL // w
    out_shape = [jax.ShapeDtypeStruct((m, w), BF16), jax.ShapeDtypeStruct((m, HEAD_DIM), F32)]

    if r > 1:
        assert r % 2 == 0
        rows = t * r
        nb = seq // rows
        cur = lambda col: pl.BlockSpec((rows, w), lambda b, i: (b * nb + i, col))
        prev = lambda col: pl.BlockSpec((rows, w), lambda b, i: (b * nb + jnp.maximum(i - 1, 0), col))
        words = pltpu.VMEM((N_HEADS_PER_DIL, rows // 2, HEAD_DIM), jnp.uint32)
        return pl.pallas_call(
            functools.partial(_dil_strided_kernel, dilation=r),
            grid=(batch, nb),
            in_specs=[cur(q_col), prev(k_col), cur(k_col), prev(v_col), cur(v_col)],
            out_specs=[pl.BlockSpec((rows, w), lambda b, i: (b * nb + i, 0)),
                       pl.BlockSpec((rows, HEAD_DIM), lambda b, i: (b * nb + i, 0))],
            out_shape=out_shape,
            scratch_shapes=[words] * 6 + [pltpu.VMEM((rows, HEAD_DIM), F32)],
            compiler_params=_params("parallel", "arbitrary"),
            name=f"dil_attention_r{r}",
        )(qkv, qkv, qkv, qkv, qkv)

    nb = seq // t
    tiles = _pick(nb, (DIL_TILES_PER_STEP, 1))
    ns = nb // tiles

    def spec(col, prev):
        if prev:
            return pl.BlockSpec((t, w), lambda b, i: (b * nb + jnp.maximum(i * tiles - 1, 0), col))
        return pl.BlockSpec((tiles * t, w), lambda b, i: (b * ns + i, col))

    out_spec = lambda width: pl.BlockSpec((tiles * t, width), lambda b, i: (b * ns + i, 0))
    return pl.pallas_call(
        _dil_attention_kernel,
        grid=(batch, ns),
        in_specs=[spec(q_col, False), spec(k_col, True), spec(k_col, False),
                  spec(v_col, True), spec(v_col, False)],
        out_specs=[out_spec(w), out_spec(HEAD_DIM)],
        out_shape=out_shape,
        compiler_params=_params("parallel", "arbitrary"),
        name="dil_attention_r1",
    )(qkv, qkv, qkv, qkv, qkv)


def _dil_merge_kernel(*refs):
    n = len(DIL_CONFIGS)
    o_refs, lse_refs, out_ref = refs[:n], refs[n:2 * n], refs[2 * n]
    lses = [r[...] for r in lse_refs]
    mx = functools.reduce(jnp.maximum, lses)
    ws = [jnp.exp(l - mx) for l in lses]
    total = sum(ws)
    alphas = [w / total for w in ws]
    for hh in range(N_HEADS_PER_DIL):
        sl = slice(hh * HEAD_DIM, (hh + 1) * HEAD_DIM)
        merged = sum(a[:, hh:hh + 1] * o[:, sl] for a, o in zip(alphas, o_refs))
        out_ref[:, sl] = merged.astype(out_ref.dtype)


def dil_merge(outs, lses):
    m, w = outs[0].shape
    tm = _pick(m, (1024, 512, 256, 128, 8))
    row = pl.BlockSpec((tm, w), lambda i: (i, 0))
    lse_row = pl.BlockSpec((tm, HEAD_DIM), lambda i: (i, 0))
    return pl.pallas_call(
        _dil_merge_kernel,
        grid=(m // tm,),
        in_specs=[row] * len(outs) + [lse_row] * len(lses),
        out_specs=row,
        out_shape=jax.ShapeDtypeStruct((m, w), BF16),
        compiler_params=_params("parallel"),
        name="dil_merge",
    )(*outs, *lses)


def _gated_proj_kernel(osb_ref, od_ref, psb_ref, pd_ref, gsb_ref, gd_ref, o_ref):
    a = jnp.dot(osb_ref[...], psb_ref[...], preferred_element_type=F32)
    b = jnp.dot(od_ref[...], pd_ref[...], preferred_element_type=F32)
    gate_sb = jax.nn.sigmoid(gsb_ref[...].astype(F32))
    gate_d = jax.nn.sigmoid(gd_ref[...].astype(F32))
    o_ref[...] = (gate_sb * a + gate_d * b).astype(o_ref.dtype)


def gated_proj(o_sb, o_d, p_sb, p_d, gates, layer, cast=None):
    m = o_sb.shape[0]
    d = p_sb.shape[-1]
    tm = _pick(m, (1024, 512, 256, 128))
    tn = _pick(d, (1024, 512, 256, 128))
    nd = d // tn
    return _grid_call(
        _gated_proj_kernel, (o_sb, o_d, p_sb, p_d, gates, gates),
        grid=(m // tm, nd),
        in_specs=[pl.BlockSpec((tm, o_sb.shape[1]), lambda i, j: (i, 0)),
                  pl.BlockSpec((tm, o_d.shape[1]), lambda i, j: (i, 0)),
                  pl.BlockSpec((None, p_sb.shape[1], tn), lambda i, j: (layer, 0, j)),
                  pl.BlockSpec((None, p_d.shape[1], tn), lambda i, j: (layer, 0, j)),
                  pl.BlockSpec((tm, tn), lambda i, j: (i, j)),
                  pl.BlockSpec((tm, tn), lambda i, j: (i, j + nd))],
        out_spec=pl.BlockSpec((tm, tn), lambda i, j: (i, j)),
        out_shape=jax.ShapeDtypeStruct((m, d), BF16),
        name="gated_proj", cast=cast)


def _ffn_close_ple_kernel(y_ref, p_ref, h_ref, gd_ref, gu_ref, wp_ref,
                          g_ffn_post_ref, g_gate_ref, g_post_ref, g_next_ref, h_out_ref, xn_out_ref):
    h = h_ref[...] + _rms(y_ref[...], g_ffn_post_ref[...])
    xn = _rms(h, g_gate_ref[...]).astype(BF16)
    low = jnp.dot(xn, gd_ref[...], preferred_element_type=F32)
    gate = jax.nn.sigmoid(jnp.dot(low.astype(BF16), gu_ref[...], preferred_element_type=F32))
    e = jnp.dot(p_ref[...].astype(BF16), wp_ref[...], preferred_element_type=F32) * gate
    h = h + _rms(e, g_post_ref[...])
    h_out_ref[...] = h
    xn_out_ref[...] = _rms(h, g_next_ref[...]).astype(xn_out_ref.dtype)


def ffn_close_ple_block(y, p, h, g_down, g_up, w_ple, g_ffn_post, g_gate, g_post, g_next,
                        layer, next_layer):
    m, d = h.shape
    dp = p.shape[-1]
    tm = _pick(m, (256, 128, 8))
    row = pl.BlockSpec((tm, d), lambda i: (i, 0))
    gain = lambda l: pl.BlockSpec((None, 1, d), lambda i: (l, 0, 0))
    return pl.pallas_call(
        _ffn_close_ple_kernel,
        grid=(m // tm,),
        in_specs=[row,
                  pl.BlockSpec((None, tm, dp), lambda i: (layer, i, 0)),
                  row,
                  pl.BlockSpec((None, d, dp), lambda i: (layer, 0, 0)),
                  pl.BlockSpec((None, dp, d), lambda i: (layer, 0, 0)),
                  pl.BlockSpec((None, dp, d), lambda i: (layer, 0, 0)),
                  gain(layer), gain(layer), gain(layer), gain(next_layer)],
        out_specs=[row, row],
        out_shape=[jax.ShapeDtypeStruct((m, d), F32), jax.ShapeDtypeStruct((m, d), BF16)],
        compiler_params=_params("parallel"),
        name="ffn_close_ple",
    )(y, p, h, g_down, g_up, w_ple, g_ffn_post, g_gate, g_post, g_next)


def kernel(x, p, w_in, w_proj_sb, w_proj_dil, w_out, g_mix_pre, g_mix_post,
           w_ffn_gate, w_ffn_up, w_ffn_down, g_ffn_pre, g_ffn_post,
           w_ple_in, w_ple_gate_down, w_ple_gate_up, g_ple_gate, g_ple_post):
    b, s, d = x.shape
    depth = w_in.shape[0]
    m = b * s
    assert w_in.shape[-1] == W_QKV + 2 * d

    bf = lambda w: w.astype(BF16)
    w_in_bf = bf(w_in[:1])
    w_proj_sb, w_proj_dil = bf(w_proj_sb), bf(w_proj_dil)
    w_ple_in, w_ple_gate_down, w_ple_gate_up = bf(w_ple_in), bf(w_ple_gate_down), bf(w_ple_gate_up)
    gains = lambda g: g.reshape(depth, 1, d)
    g_mix_pre, g_mix_post, g_ffn_pre, g_ffn_post, g_ple_gate, g_ple_post = map(
        gains, (g_mix_pre, g_mix_post, g_ffn_pre, g_ffn_post, g_ple_gate, g_ple_post))
    p = p.reshape(depth, m, p.shape[-1])

    h = x.reshape(m, d)
    xn = rmsnorm_bf16(h, g_mix_pre, 0)
    for i in range(depth):
        qkv = matmul(xn, w_in_bf, 0, col0=0, n_out=W_QKV, out_dtype=BF16, name="in_proj_qkv")
        gates, w_gate_bf = matmul(xn, w_in_bf, 0, col0=W_QKV, n_out=2 * d, out_dtype=BF16,
                                  name="in_proj_gates", cast=(w_ffn_gate, i))
        o_sb = sb_attention(qkv, b, s)
        groups = [dil_attention_group(qkv, b, s, g) for g in range(len(DIL_CONFIGS))]
        o_d = dil_merge([o for o, _ in groups], [l for _, l in groups])
        merged, w_out_bf = gated_proj(o_sb, o_d, w_proj_sb, w_proj_dil, gates, i, cast=(w_out, i))
        y, w_up_bf = matmul(merged, w_out_bf, 0, col0=0, n_out=d, out_dtype=F32, name="out_proj",
                            cast=(w_ffn_up, i))
        h, xn = norm_residual(y, h, g_mix_post, g_ffn_pre, i, i)
        hidden, w_down_bf = swiglu_hidden(xn, w_gate_bf, w_up_bf, 0, cast=(w_ffn_down, i))
        if i + 1 < depth:
            y, w_in_bf = matmul(hidden, w_down_bf, 0, col0=0, n_out=d, out_dtype=F32,
                                name="ffn_down", cast=(w_in, i + 1))
        else:
            y = matmul(hidden, w_down_bf, 0, col0=0, n_out=d, out_dtype=F32, name="ffn_down")
        h, xn = ffn_close_ple_block(y, p, h, w_ple_gate_down, w_ple_gate_up, w_ple_in,
                                    g_ffn_post, g_ple_gate, g_ple_post, g_mix_pre, i, (i + 1) % depth)
    return h.reshape(b, s, d)
```

```python
import functools

import jax
import jax.numpy as jnp
from jax import lax
from jax.experimental import pallas as pl
from jax.experimental.pallas import tpu as pltpu

HEAD_DIM = 128
N_HEADS_SB = 8
DIL_CONFIGS = ((128, 1), (512, 4), (2048, 16))
N_HEADS_PER_DIL = 4
EPS = 1e-6

W_SB = N_HEADS_SB * HEAD_DIM
W_DIL_GROUP = N_HEADS_PER_DIL * HEAD_DIM
W_DIL = W_DIL_GROUP * len(DIL_CONFIGS)
W_QKV = 3 * W_SB + 3 * W_DIL
ATT_TILE = DIL_CONFIGS[0][0] // DIL_CONFIGS[0][1]
assert all(w // r == ATT_TILE for w, r in DIL_CONFIGS)
assert ATT_TILE == HEAD_DIM

F32_EXP_UNDERFLOW = 104.0
SB_WINDOW_TILES = 3
SB_Q_TILES_PER_STEP = 4
SB_Q_TILES_PER_SUFFIX_DOT = 2
DIL_TILES_PER_STEP = 2

V7X_VMEM_BYTES = 64 * 1024 * 1024
VMEM_LIMIT_BYTES = V7X_VMEM_BYTES - 6 * 1024 * 1024

BF16 = jnp.bfloat16
F32 = jnp.float32
BF16_SUBLANES = 16


def _params(*sem):
    return pltpu.CompilerParams(dimension_semantics=sem, vmem_limit_bytes=VMEM_LIMIT_BYTES)


def _pick(n, candidates):
    for c in candidates:
        if n % c == 0:
            return c
    raise ValueError(f"no tile in {candidates} divides {n}")


def _matmul_tiles(m, k, n, out_bytes, n_weights=1, col0=0):
    budget = VMEM_LIMIT_BYTES - 8 * 1024 * 1024
    for tm, tn in ((1024, 1280), (1024, 1024), (2048, 512), (1024, 768), (2048, 256), (1024, 512),
                   (1024, 256), (512, 512), (512, 256), (256, 256), (256, 128), (128, 128)):
        if m % tm or n % tn or col0 % tn:
            continue
        need = 2 * (tm * k * 2 + n_weights * k * tn * 2 + tm * tn * out_bytes) + n_weights * tm * tn * 4
        if need <= budget:
            return tm, tn
    raise ValueError(f"no matmul tiling for {(m, k, n)}")


def _rms(x, g):
    return x * lax.rsqrt(jnp.mean(x * x, axis=-1, keepdims=True) + EPS) * g


def _rmsnorm_kernel(x_ref, g_ref, o_ref):
    o_ref[...] = _rms(x_ref[...], g_ref[...]).astype(o_ref.dtype)


def rmsnorm_bf16(x, g, layer):
    m, d = x.shape
    tm = _pick(m, (512, 256, 128, 8))
    return pl.pallas_call(
        _rmsnorm_kernel,
        grid=(m // tm,),
        in_specs=[pl.BlockSpec((tm, d), lambda i: (i, 0)),
                  pl.BlockSpec((None, 1, d), lambda i: (layer, 0, 0))],
        out_specs=pl.BlockSpec((tm, d), lambda i: (i, 0)),
        out_shape=jax.ShapeDtypeStruct((m, d), BF16),
        compiler_params=_params("parallel"),
        name="rmsnorm",
    )(x, g)


def _cast_kernel(w_ref, o_ref):
    o_ref[...] = w_ref[...].astype(o_ref.dtype)


def cast_layer_bf16(w, layer):
    _, r, c = w.shape
    rows = _pick(r, (128, 64, 32, 16))
    return pl.pallas_call(
        _cast_kernel,
        grid=(r // rows,),
        in_specs=[pl.BlockSpec((None, rows, c), lambda i: (layer, i, 0))],
        out_specs=pl.BlockSpec((None, rows, c), lambda i: (0, i, 0)),
        out_shape=jax.ShapeDtypeStruct((1, r, c), BF16),
        compiler_params=_params("parallel"),
        name="cast_bf16",
    )(w)


def _cast_rows(total_rows, steps):
    for rows in range(BF16_SUBLANES, total_rows + 1, BF16_SUBLANES):
        if total_rows % rows == 0 and total_rows // rows <= steps:
            return rows
    raise ValueError(f"cannot cover {total_rows} rows in {steps} steps")


def _grid_call(body, args, *, grid, in_specs, out_spec, out_shape, name, cast=None):
    if cast is None:
        return pl.pallas_call(
            body, grid=grid, in_specs=in_specs, out_specs=out_spec, out_shape=out_shape,
            compiler_params=_params("parallel", "parallel"), name=name)(*args)
    src, layer = cast
    _, r, c = src.shape
    rows = _cast_rows(r, grid[0] * grid[1])
    blk = lambda i, j: jnp.minimum(i * grid[1] + j, r // rows - 1)
    n_in = len(in_specs)

    def body_with_cast(*refs):
        body(*refs[:n_in], refs[n_in + 1])
        refs[n_in + 2][...] = refs[n_in][...].astype(BF16)

    return pl.pallas_call(
        body_with_cast, grid=grid,
        in_specs=[*in_specs, pl.BlockSpec((None, rows, c), lambda i, j: (layer, blk(i, j), 0))],
        out_specs=[out_spec, pl.BlockSpec((None, rows, c), lambda i, j: (0, blk(i, j), 0))],
        out_shape=[out_shape, jax.ShapeDtypeStruct((1, r, c), BF16)],
        compiler_params=_params("arbitrary", "arbitrary"), name=name)(*args, src)


def _matmul_kernel(x_ref, w_ref, o_ref):
    o_ref[...] = jnp.dot(x_ref[...], w_ref[...], preferred_element_type=F32).astype(o_ref.dtype)


def matmul(x, w, layer, *, col0, n_out, out_dtype, name, cast=None):
    m, k = x.shape
    tm, tn = _matmul_tiles(m, k, n_out, jnp.dtype(out_dtype).itemsize, col0=col0)
    c0 = col0 // tn
    return _grid_call(
        _matmul_kernel, (x, w),
        grid=(m // tm, n_out // tn),
        in_specs=[pl.BlockSpec((tm, k), lambda i, j: (i, 0)),
                  pl.BlockSpec((None, k, tn), lambda i, j: (layer, 0, j + c0))],
        out_spec=pl.BlockSpec((tm, tn), lambda i, j: (i, j)),
        out_shape=jax.ShapeDtypeStruct((m, n_out), out_dtype),
        name=name, cast=cast)


def _swiglu_kernel(x_ref, wg_ref, wu_ref, o_ref):
    x = x_ref[...]
    g = jnp.dot(x, wg_ref[...], preferred_element_type=F32)
    u = jnp.dot(x, wu_ref[...], preferred_element_type=F32)
    o_ref[...] = (g * jax.nn.sigmoid(g) * u).astype(o_ref.dtype)


def swiglu_hidden(x, wg, wu, layer, cast=None):
    m, k = x.shape
    n = wg.shape[-1]
    tm, tn = _matmul_tiles(m, k, n, 2, n_weights=2)
    return _grid_call(
        _swiglu_kernel, (x, wg, wu),
        grid=(m // tm, n // tn),
        in_specs=[pl.BlockSpec((tm, k), lambda i, j: (i, 0)),
                  pl.BlockSpec((None, k, tn), lambda i, j: (layer, 0, j)),
                  pl.BlockSpec((None, k, tn), lambda i, j: (layer, 0, j))],
        out_spec=pl.BlockSpec((tm, tn), lambda i, j: (i, j)),
        out_shape=jax.ShapeDtypeStruct((m, n), BF16),
        name="swiglu_hidden", cast=cast)


def _norm_residual_kernel(y_ref, h_ref, gpost_ref, gnext_ref, h_out_ref, xn_out_ref):
    h = h_ref[...] + _rms(y_ref[...], gpost_ref[...])
    h_out_ref[...] = h
    xn_out_ref[...] = _rms(h, gnext_ref[...]).astype(xn_out_ref.dtype)


def norm_residual(y, h, g_post, g_next, layer, next_layer):
    m, d = h.shape
    tm = _pick(m, (256, 128, 8))
    row = pl.BlockSpec((tm, d), lambda i: (i, 0))
    return pl.pallas_call(
        _norm_residual_kernel,
        grid=(m // tm,),
        in_specs=[row, row,
                  pl.BlockSpec((None, 1, d), lambda i: (layer, 0, 0)),
                  pl.BlockSpec((None, 1, d), lambda i: (next_layer, 0, 0))],
        out_specs=[row, row],
        out_shape=[jax.ShapeDtypeStruct((m, d), F32), jax.ShapeDtypeStruct((m, d), BF16)],
        compiler_params=_params("parallel"),
        name="norm_residual",
    )(y, h, g_post, g_next)


def _sb_scores(q, k, q0, k0):
    z = lax.dot_general(q, k, (((1,), (1,)), ((), ())), preferred_element_type=F32) * HEAD_DIM ** -0.5
    row = lax.broadcasted_iota(jnp.int32, z.shape, 0)
    col = lax.broadcasted_iota(jnp.int32, z.shape, 1)
    causal = (col - row) < (q0 - k0)
    log_beta = jnp.minimum(z, 0.0) - jnp.log(1.0 + jnp.exp(-jnp.abs(z)))
    log_stay = jnp.where(causal, log_beta - z, 0.0)
    return log_beta, log_stay, causal


def _suffix_sums(tiles, tri2):
    t = ATT_TILE
    x = jnp.concatenate(tiles, axis=0) if len(tiles) > 1 else tiles[0]
    hi = x.astype(BF16)
    lo = (x - hi.astype(F32)).astype(BF16)
    sums = jnp.dot(jnp.concatenate([hi, lo], axis=1), tri2, preferred_element_type=F32)
    return [(sums[n * t:(n + 1) * t, :t], sums[n * t:(n + 1) * t, t:]) for n in range(len(tiles))]


def _sb_attention_kernel(q_ref, k_ref, v_ref, tri2_ref, o_ref):
    s_len = q_ref.shape[0]
    t = ATT_TILE
    nw = SB_WINDOW_TILES
    wk = nw * t

    def windows(tiles):
        pre = []
        for i in tiles:
            q0 = pl.multiple_of(i * t, t)
            first = jnp.maximum(i - (nw - 1), 0)
            k0 = pl.multiple_of(first * t, t)
            q = q_ref[pl.ds(q0, t), :]
            pre.append((q, q0, first, k0) + _sb_scores(q, k_ref[pl.ds(k0, wk), :], q0, k0))
        sums = []
        for c in range(0, len(pre), SB_Q_TILES_PER_SUFFIX_DOT):
            part = pre[c:c + SB_Q_TILES_PER_SUFFIX_DOT]
            sums += _suffix_sums([p[5][:, d * t:(d + 1) * t] for p in part for d in range(nw)], tri2_ref[...])
        out = []
        for u, (q, q0, first, k0, log_beta, _, causal) in enumerate(pre):
            after, run = [], jnp.zeros((t, t), F32)
            for d in reversed(range(nw)):
                cum, tot = sums[u * nw + d]
                after.append(cum + run)
                run = run + tot
            after = jnp.concatenate(after[::-1], axis=1)
            w = jnp.where(causal, jnp.exp(log_beta + after), 0.0)
            acc = jnp.dot(w.astype(BF16), v_ref[pl.ds(k0, wk), :], preferred_element_type=F32)
            out.append((q, q0, first - 1, run, acc))
        return out

    def more(j, run):
        return jnp.logical_and(j >= 0, jnp.max(run) > -F32_EXP_UNDERFLOW)

    def tail(q, q0, j, run, acc):
        def k_tile(state):
            j, _, run, acc = state
            k0 = pl.multiple_of(j * t, t)
            log_beta, log_stay, _ = _sb_scores(q, k_ref[pl.ds(k0, t), :], q0, k0)
            (cum, tot), = _suffix_sums([log_stay], tri2_ref[...])
            w = jnp.exp(log_beta + run + cum)
            acc = acc + jnp.dot(w.astype(BF16), v_ref[pl.ds(k0, t), :], preferred_element_type=F32)
            run = run + tot
            return j - 1, more(j - 1, run), run, acc

        return lax.while_loop(lambda st: st[1], k_tile, (j, more(j, run), run, acc))[3]

    def q_group(g, carry):
        wins = windows([g * SB_Q_TILES_PER_STEP + u for u in range(SB_Q_TILES_PER_STEP)])
        for q, q0, j, run, acc in wins:
            o_ref[pl.ds(q0, t), :] = tail(q, q0, j, run, acc).astype(o_ref.dtype)
        return carry

    lax.fori_loop(0, s_len // (t * SB_Q_TILES_PER_STEP), q_group, 0)


def _suffix_sum_matrix():
    r = jnp.arange(ATT_TILE)
    strict_lower = (r[:, None] > r[None, :]).astype(BF16)
    half = jnp.concatenate([strict_lower, jnp.ones((ATT_TILE, ATT_TILE), BF16)], axis=1)
    return jnp.concatenate([half, half], axis=0)


def sb_attention(qkv, batch, seq):
    m = qkv.shape[0]
    h = N_HEADS_SB
    assert seq % (ATT_TILE * SB_Q_TILES_PER_STEP) == 0 and seq >= SB_WINDOW_TILES * ATT_TILE
    blk = lambda off: pl.BlockSpec((seq, HEAD_DIM), lambda b, n: (b, n + off))
    tri2 = _suffix_sum_matrix()
    return pl.pallas_call(
        _sb_attention_kernel,
        grid=(batch, h),
        in_specs=[blk(0), blk(h), blk(2 * h), pl.BlockSpec(tri2.shape, lambda b, n: (0, 0))],
        out_specs=blk(0),
        out_shape=jax.ShapeDtypeStruct((m, W_SB), BF16),
        compiler_params=_params("parallel", "parallel"),
        name="sb_attention",
    )(qkv, qkv, qkv, tri2)


def _dil_attention_kernel(q_ref, kp_ref, kc_ref, vp_ref, vc_ref, o_ref, lse_ref):
    t = ATT_TILE
    i = pl.program_id(1)
    in_band, own_tile, dist = _dil_band(1)
    lane = lax.broadcasted_iota(jnp.int32, (t, HEAD_DIM), 1)
    for a in range(q_ref.shape[0] // t):
        rows = slice(a * t, (a + 1) * t)
        prev = slice((a - 1) * t, a * t)
        valid = in_band if a else jnp.logical_and(in_band, jnp.logical_or(own_tile, i > 0))
        lse = jnp.zeros((t, HEAD_DIM), F32)
        for hh in range(N_HEADS_PER_DIL):
            sl = slice(hh * HEAD_DIM, (hh + 1) * HEAD_DIM)
            k = jnp.concatenate([kc_ref[prev, sl] if a else kp_ref[:, sl], kc_ref[rows, sl]], axis=0)
            v = jnp.concatenate([vc_ref[prev, sl] if a else vp_ref[:, sl], vc_ref[rows, sl]], axis=0)
            out, lse_h = _dil_tile(q_ref[rows, sl], k, v, valid, dist, hh)
            o_ref[rows, sl] = out.astype(o_ref.dtype)
            lse = jnp.where(lane == hh, lse_h, lse)
        lse_ref[rows, :] = lse


def _dil_tile(q, k, v, valid, dist, head):
    slope = 2.0 ** (-8.0 * (head + 1) / N_HEADS_PER_DIL)
    s = lax.dot_general(q, k, (((1,), (1,)), ((), ())), preferred_element_type=F32) * HEAD_DIM ** -0.5
    s = jnp.where(valid, s - slope * dist, -jnp.inf)
    mx = jnp.max(s, axis=1, keepdims=True)
    e = jnp.exp(s - mx).astype(BF16)
    v_ones = jnp.concatenate([v, jnp.ones_like(v)], axis=1)
    pv_den = jnp.dot(e, v_ones, preferred_element_type=F32)
    pv, den = pv_den[:, :HEAD_DIM], pv_den[:, HEAD_DIM:]
    return pv / den, mx + jnp.log(den[:, :1])


def _dil_band(dilation):
    t = ATT_TILE
    row = lax.broadcasted_iota(jnp.int32, (t, 2 * t), 0)
    col = lax.broadcasted_iota(jnp.int32, (t, 2 * t), 1)
    steps = t + row - col
    return jnp.logical_and(steps >= 0, steps <= t), col >= t, (steps * dilation).astype(F32)


def _dil_strided_kernel(q_ref, kp_ref, kc_ref, vp_ref, vc_ref, o_ref, lse_ref,
                        qw, kpw, kcw, vpw, vcw, ow, lse_rows, *, dilation):
    t = ATT_TILE
    r = dilation
    half = r // 2
    heads = N_HEADS_PER_DIL
    i = pl.program_id(1)
    for src, words in ((q_ref, qw), (kp_ref, kpw), (kc_ref, kcw), (vp_ref, vpw), (vc_ref, vcw)):
        for hh in range(heads):
            words[hh] = pltpu.bitcast(src[:, hh * HEAD_DIM:(hh + 1) * HEAD_DIM], jnp.uint32)
    in_band, own_tile, dist = _dil_band(r)
    valid = jnp.logical_and(in_band, jnp.logical_or(own_tile, i > 0))
    lane = lax.broadcasted_iota(jnp.int32, (t, HEAD_DIM), 1)
    high = jnp.uint32(0xFFFF0000)

    def unpack(words, odd):
        return pltpu.bitcast(words & high if odd else words << 16, F32).astype(BF16)

    def bf16_bits(x):
        return pltpu.bitcast(x.astype(BF16).astype(F32), jnp.uint32)

    def class_pair(p, carry):
        sel = pl.ds(p, t, stride=half)
        lses = [jnp.zeros((t, HEAD_DIM), F32)] * 2
        for hh in range(heads):
            wq, wkp, wkc, wvp, wvc = (w[hh, sel, :] for w in (qw, kpw, kcw, vpw, vcw))
            outs = []
            for odd in (0, 1):
                k = jnp.concatenate([unpack(wkp, odd), unpack(wkc, odd)], axis=0)
                v = jnp.concatenate([unpack(wvp, odd), unpack(wvc, odd)], axis=0)
                out, lse_h = _dil_tile(unpack(wq, odd), k, v, valid, dist, hh)
                outs.append(out)
                lses[odd] = jnp.where(lane == hh, lse_h, lses[odd])
            ow[hh, sel, :] = (bf16_bits(outs[0]) >> 16) | (bf16_bits(outs[1]) & high)
        for odd in (0, 1):
            lse_rows[pl.ds(2 * p + odd, t, stride=r), :] = lses[odd]
        return carry

    lax.fori_loop(0, half, class_pair, 0)
    for hh in range(heads):
        o_ref[:, hh * HEAD_DIM:(hh + 1) * HEAD_DIM] = pltpu.bitcast(ow[hh], BF16)
    lse_ref[...] = lse_rows[...]


def dil_attention_group(qkv, batch, seq, group):
    m, wq = qkv.shape
    _, r = DIL_CONFIGS[group]
    t = ATT_TILE
    w = W_DIL_GROUP
    assert wq % w == 0 and seq % (t * r) == 0
    q_col = (3 * W_SB) // w + group
    k_col = q_col + W_DIL // w
    v_col = k_col + W_DIL // w
    out_shape = [jax.ShapeDtypeStruct((m, w), BF16), jax.ShapeDtypeStruct((m, HEAD_DIM), F32)]

    if r > 1:
        assert r % 2 == 0
        rows = t * r
        nb = seq // rows
        cur = lambda col: pl.BlockSpec((rows, w), lambda b, i: (b * nb + i, col))
        prev = lambda col: pl.BlockSpec((rows, w), lambda b, i: (b * nb + jnp.maximum(i - 1, 0), col))
        words = pltpu.VMEM((N_HEADS_PER_DIL, rows // 2, HEAD_DIM), jnp.uint32)
        return pl.pallas_call(
            functools.partial(_dil_strided_kernel, dilation=r),
            grid=(batch, nb),
            in_specs=[cur(q_col), prev(k_col), cur(k_col), prev(v_col), cur(v_col)],
            out_specs=[pl.BlockSpec((rows, w), lambda b, i: (b * nb + i, 0)),
                       pl.BlockSpec((rows, HEAD_DIM), lambda b, i: (b * nb + i, 0))],
            out_shape=out_shape,
            scratch_shapes=[words] * 6 + [pltpu.VMEM((rows, HEAD_DIM), F32)],
            compiler_params=_params("parallel", "arbitrary"),
            name=f"dil_attention_r{r}",
        )(qkv, qkv, qkv, qkv, qkv)

    nb = seq // t
    tiles = _pick(nb, (DIL_TILES_PER_STEP, 1))
    ns = nb // tiles

    def spec(col, prev):
        if prev:
            return pl.BlockSpec((t, w), lambda b, i: (b * nb + jnp.maximum(i * tiles - 1, 0), col))
        return pl.BlockSpec((tiles * t, w), lambda b, i: (b * ns + i, col))

    out_spec = lambda width: pl.BlockSpec((tiles * t, width), lambda b, i: (b * ns + i, 0))
    return pl.pallas_call(
        _dil_attention_kernel,
        grid=(batch, ns),
        in_specs=[spec(q_col, False), spec(k_col, True), spec(k_col, False),
                  spec(v_col, True), spec(v_col, False)],
        out_specs=[out_spec(w), out_spec(HEAD_DIM)],
        out_shape=out_shape,
        compiler_params=_params("parallel", "arbitrary"),
        name="dil_attention_r1",
    )(qkv, qkv, qkv, qkv, qkv)


def _dil_merge_kernel(*refs):
    n = len(DIL_CONFIGS)
    o_refs, lse_refs, out_ref = refs[:n], refs[n:2 * n], refs[2 * n]
    lses = [r[...] for r in lse_refs]
    mx = functools.reduce(jnp.maximum, lses)
    ws = [jnp.exp(l - mx) for l in lses]
    total = sum(ws)
    alphas = [w / total for w in ws]
    for hh in range(N_HEADS_PER_DIL):
        sl = slice(hh * HEAD_DIM, (hh + 1) * HEAD_DIM)
        merged = sum(a[:, hh:hh + 1] * o[:, sl] for a, o in zip(alphas, o_refs))
        out_ref[:, sl] = merged.astype(out_ref.dtype)


def dil_merge(outs, lses):
    m, w = outs[0].shape
    tm = _pick(m, (1024, 512, 256, 128, 8))
    row = pl.BlockSpec((tm, w), lambda i: (i, 0))
    lse_row = pl.BlockSpec((tm, HEAD_DIM), lambda i: (i, 0))
    return pl.pallas_call(
        _dil_merge_kernel,
        grid=(m // tm,),
        in_specs=[row] * len(outs) + [lse_row] * len(lses),
        out_specs=row,
        out_shape=jax.ShapeDtypeStruct((m, w), BF16),
        compiler_params=_params("parallel"),
        name="dil_merge",
    )(*outs, *lses)


def _gated_proj_kernel(osb_ref, od_ref, psb_ref, pd_ref, gsb_ref, gd_ref, o_ref):
    a = jnp.dot(osb_ref[...], psb_ref[...], preferred_element_type=F32)
    b = jnp.dot(od_ref[...], pd_ref[...], preferred_element_type=F32)
    gate_sb = jax.nn.sigmoid(gsb_ref[...].astype(F32))
    gate_d = jax.nn.sigmoid(gd_ref[...].astype(F32))
    o_ref[...] = (gate_sb * a + gate_d * b).astype(o_ref.dtype)


def gated_proj(o_sb, o_d, p_sb, p_d, gates, layer, cast=None):
    m = o_sb.shape[0]
    d = p_sb.shape[-1]
    tm = _pick(m, (1024, 512, 256, 128))
    tn = _pick(d, (1024, 512, 256, 128))
    nd = d // tn
    return _grid_call(
        _gated_proj_kernel, (o_sb, o_d, p_sb, p_d, gates, gates),
        grid=(m // tm, nd),
        in_specs=[pl.BlockSpec((tm, o_sb.shape[1]), lambda i, j: (i, 0)),
                  pl.BlockSpec((tm, o_d.shape[1]), lambda i, j: (i, 0)),
                  pl.BlockSpec((None, p_sb.shape[1], tn), lambda i, j: (layer, 0, j)),
                  pl.BlockSpec((None, p_d.shape[1], tn), lambda i, j: (layer, 0, j)),
                  pl.BlockSpec((tm, tn), lambda i, j: (i, j)),
                  pl.BlockSpec((tm, tn), lambda i, j: (i, j + nd))],
        out_spec=pl.BlockSpec((tm, tn), lambda i, j: (i, j)),
        out_shape=jax.ShapeDtypeStruct((m, d), BF16),
        name="gated_proj", cast=cast)


def _ffn_close_ple_kernel(y_ref, p_ref, h_ref, gd_ref, gu_ref, wp_ref,
                          g_ffn_post_ref, g_gate_ref, g_post_ref, g_next_ref, h_out_ref, xn_out_ref):
    h = h_ref[...] + _rms(y_ref[...], g_ffn_post_ref[...])
    xn = _rms(h, g_gate_ref[...]).astype(BF16)
    low = jnp.dot(xn, gd_ref[...], preferred_element_type=F32)
    gate = jax.nn.sigmoid(jnp.dot(low.astype(BF16), gu_ref[...], preferred_element_type=F32))
    e = jnp.dot(p_ref[...].astype(BF16), wp_ref[...], preferred_element_type=F32) * gate
    h = h + _rms(e, g_post_ref[...])
    h_out_ref[...] = h
    xn_out_ref[...] = _rms(h, g_next_ref[...]).astype(xn_out_ref.dtype)


def ffn_close_ple_block(y, p, h, g_down, g_up, w_ple, g_ffn_post, g_gate, g_post, g_next,
                        layer, next_layer):
    m, d = h.shape
    dp = p.shape[-1]
    tm = _pick(m, (256, 128, 8))
    row = pl.BlockSpec((tm, d), lambda i: (i, 0))
    gain = lambda l: pl.BlockSpec((None, 1, d), lambda i: (l, 0, 0))
    return pl.pallas_call(
        _ffn_close_ple_kernel,
        grid=(m // tm,),
        in_specs=[row,
                  pl.BlockSpec((None, tm, dp), lambda i: (layer, i, 0)),
                  row,
                  pl.BlockSpec((None, d, dp), lambda i: (layer, 0, 0)),
                  pl.BlockSpec((None, dp, d), lambda i: (layer, 0, 0)),
                  pl.BlockSpec((None, dp, d), lambda i: (layer, 0, 0)),
                  gain(layer), gain(layer), gain(layer), gain(next_layer)],
        out_specs=[row, row],
        out_shape=[jax.ShapeDtypeStruct((m, d), F32), jax.ShapeDtypeStruct((m, d), BF16)],
        compiler_params=_params("parallel"),
        name="ffn_close_ple",
    )(y, p, h, g_down, g_up, w_ple, g_ffn_post, g_gate, g_post, g_next)


def kernel(x, p, w_in, w_proj_sb, w_proj_dil, w_out, g_mix_pre, g_mix_post,
           w_ffn_gate, w_ffn_up, w_ffn_down, g_ffn_pre, g_ffn_post,
           w_ple_in, w_ple_gate_down, w_ple_gate_up, g_ple_gate, g_ple_post):
    b, s, d = x.shape
    depth = w_in.shape[0]
    m = b * s
    assert w_in.shape[-1] == W_QKV + 2 * d

    bf = lambda w: w.astype(BF16)
    w_in_bf = cast_layer_bf16(w_in, 0)
    w_proj_sb, w_proj_dil = bf(w_proj_sb), bf(w_proj_dil)
    w_ple_in, w_ple_gate_down, w_ple_gate_up = bf(w_ple_in), bf(w_ple_gate_down), bf(w_ple_gate_up)
    gains = lambda g: g.reshape(depth, 1, d)
    g_mix_pre, g_mix_post, g_ffn_pre, g_ffn_post, g_ple_gate, g_ple_post = map(
        gains, (g_mix_pre, g_mix_post, g_ffn_pre, g_ffn_post, g_ple_gate, g_ple_post))
    p = p.reshape(depth, m, p.shape[-1])

    h = x.reshape(m, d)
    xn = rmsnorm_bf16(h, g_mix_pre, 0)
    for i in range(depth):
        qkv = matmul(xn, w_in_bf, 0, col0=0, n_out=W_QKV, out_dtype=BF16, name="in_proj_qkv")
        gates, w_gate_bf = matmul(xn, w_in_bf, 0, col0=W_QKV, n_out=2 * d, out_dtype=BF16,
                                  name="in_proj_gates", cast=(w_ffn_gate, i))
        o_sb = sb_attention(qkv, b, s)
        groups = [dil_attention_group(qkv, b, s, g) for g in range(len(DIL_CONFIGS))]
        o_d = dil_merge([o for o, _ in groups], [l for _, l in groups])
        merged, w_out_bf = gated_proj(o_sb, o_d, w_proj_sb, w_proj_dil, gates, i, cast=(w_out, i))
        y, w_up_bf = matmul(merged, w_out_bf, 0, col0=0, n_out=d, out_dtype=F32, name="out_proj",
                            cast=(w_ffn_up, i))
        h, xn = norm_residual(y, h, g_mix_post, g_ffn_pre, i, i)
        hidden, w_down_bf = swiglu_hidden(xn, w_gate_bf, w_up_bf, 0, cast=(w_ffn_down, i))
        if i + 1 < depth:
            y, w_in_bf = matmul(hidden, w_down_bf, 0, col0=0, n_out=d, out_dtype=F32,
                                name="ffn_down", cast=(w_in, i + 1))
        else:
            y = matmul(hidden, w_down_bf, 0, col0=0, n_out=d, out_dtype=F32, name="ffn_down")
        h, xn = ffn_close_ple_block(y, p, h, w_ple_gate_down, w_ple_gate_up, w_ple_in,
                                    g_ffn_post, g_ple_gate, g_ple_post, g_mix_pre, i, (i + 1) % depth)
    return h.reshape(b, s, d)
```

```python
import functools

import jax
import jax.numpy as jnp
from jax import lax
from jax.experimental import pallas as pl
from jax.experimental.pallas import tpu as pltpu

HEAD_DIM = 128
N_HEADS_SB = 8
DIL_CONFIGS = ((128, 1), (512, 4), (2048, 16))
N_HEADS_PER_DIL = 4
EPS = 1e-6

W_SB = N_HEADS_SB * HEAD_DIM
W_DIL_GROUP = N_HEADS_PER_DIL * HEAD_DIM
W_DIL = W_DIL_GROUP * len(DIL_CONFIGS)
W_QKV = 3 * W_SB + 3 * W_DIL
ATT_TILE = DIL_CONFIGS[0][0] // DIL_CONFIGS[0][1]
assert all(w // r == ATT_TILE for w, r in DIL_CONFIGS)
assert ATT_TILE == HEAD_DIM

F32_EXP_UNDERFLOW = 104.0
SB_WINDOW_TILES = 3
SB_Q_TILES_PER_STEP = 4
SB_Q_TILES_PER_SUFFIX_DOT = 2
DIL_TILES_PER_STEP = 2

V7X_VMEM_BYTES = 64 * 1024 * 1024
VMEM_LIMIT_BYTES = V7X_VMEM_BYTES - 6 * 1024 * 1024

BF16 = jnp.bfloat16
F32 = jnp.float32
BF16_SUBLANES = 16


def _params(*sem):
    return pltpu.CompilerParams(dimension_semantics=sem, vmem_limit_bytes=VMEM_LIMIT_BYTES)


def _pick(n, candidates):
    for c in candidates:
        if n % c == 0:
            return c
    raise ValueError(f"no tile in {candidates} divides {n}")


def _matmul_tiles(m, k, n, out_bytes, n_weights=1, col0=0):
    budget = VMEM_LIMIT_BYTES - 8 * 1024 * 1024
    for tm, tn in ((1024, 1280), (1024, 1024), (2048, 512), (1024, 768), (2048, 256), (1024, 512),
                   (1024, 256), (512, 512), (512, 256), (256, 256), (256, 128), (128, 128)):
        if m % tm or n % tn or col0 % tn:
            continue
        need = 2 * (tm * k * 2 + n_weights * k * tn * 2 + tm * tn * out_bytes) + n_weights * tm * tn * 4
        if need <= budget:
            return tm, tn
    raise ValueError(f"no matmul tiling for {(m, k, n)}")


def _rms(x, g):
    return x * lax.rsqrt(jnp.mean(x * x, axis=-1, keepdims=True) + EPS) * g


def _rmsnorm_kernel(x_ref, g_ref, o_ref):
    o_ref[...] = _rms(x_ref[...], g_ref[...]).astype(o_ref.dtype)


def rmsnorm_bf16(x, g, layer):
    m, d = x.shape
    tm = _pick(m, (512, 256, 128, 8))
    return pl.pallas_call(
        _rmsnorm_kernel,
        grid=(m // tm,),
        in_specs=[pl.BlockSpec((tm, d), lambda i: (i, 0)),
                  pl.BlockSpec((None, 1, d), lambda i: (layer, 0, 0))],
        out_specs=pl.BlockSpec((tm, d), lambda i: (i, 0)),
        out_shape=jax.ShapeDtypeStruct((m, d), BF16),
        compiler_params=_params("parallel"),
        name="rmsnorm",
    )(x, g)


def _cast_kernel(w_ref, o_ref):
    o_ref[...] = w_ref[...].astype(o_ref.dtype)


def cast_layer_bf16(w, layer):
    _, r, c = w.shape
    rows = _pick(r, (128, 64, 32, 16))
    return pl.pallas_call(
        _cast_kernel,
        grid=(r // rows,),
        in_specs=[pl.BlockSpec((None, rows, c), lambda i: (layer, i, 0))],
        out_specs=pl.BlockSpec((None, rows, c), lambda i: (0, i, 0)),
        out_shape=jax.ShapeDtypeStruct((1, r, c), BF16),
        compiler_params=_params("parallel"),
        name="cast_bf16",
    )(w)


def _cast_rows(total_rows, steps):
    for rows in range(BF16_SUBLANES, total_rows + 1, BF16_SUBLANES):
        if total_rows % rows == 0 and total_rows // rows <= steps:
            return rows
    raise ValueError(f"cannot cover {total_rows} rows in {steps} steps")


def _grid_call(body, args, *, grid, in_specs, out_spec, out_shape, name, cast=None):
    if cast is None:
        return pl.pallas_call(
            body, grid=grid, in_specs=in_specs, out_specs=out_spec, out_shape=out_shape,
            compiler_params=_params("parallel", "parallel"), name=name)(*args)
    src, layer = cast
    _, r, c = src.shape
    rows = _cast_rows(r, grid[0] * grid[1])
    blk = lambda i, j: jnp.minimum(i * grid[1] + j, r // rows - 1)
    n_in = len(in_specs)

    def body_with_cast(*refs):
        body(*refs[:n_in], refs[n_in + 1])
        refs[n_in + 2][...] = refs[n_in][...].astype(BF16)

    return pl.pallas_call(
        body_with_cast, grid=grid,
        in_specs=[*in_specs, pl.BlockSpec((None, rows, c), lambda i, j: (layer, blk(i, j), 0))],
        out_specs=[out_spec, pl.BlockSpec((None, rows, c), lambda i, j: (0, blk(i, j), 0))],
        out_shape=[out_shape, jax.ShapeDtypeStruct((1, r, c), BF16)],
        compiler_params=_params("arbitrary", "arbitrary"), name=name)(*args, src)


def _matmul_kernel(x_ref, w_ref, o_ref):
    o_ref[...] = jnp.dot(x_ref[...], w_ref[...], preferred_element_type=F32).astype(o_ref.dtype)


def matmul(x, w, layer, *, col0, n_out, out_dtype, name, cast=None):
    m, k = x.shape
    tm, tn = _matmul_tiles(m, k, n_out, jnp.dtype(out_dtype).itemsize, col0=col0)
    c0 = col0 // tn
    return _grid_call(
        _matmul_kernel, (x, w),
        grid=(m // tm, n_out // tn),
        in_specs=[pl.BlockSpec((tm, k), lambda i, j: (i, 0)),
                  pl.BlockSpec((None, k, tn), lambda i, j: (layer, 0, j + c0))],
        out_spec=pl.BlockSpec((tm, tn), lambda i, j: (i, j)),
        out_shape=jax.ShapeDtypeStruct((m, n_out), out_dtype),
        name=name, cast=cast)


def _swiglu_kernel(x_ref, wg_ref, wu_ref, o_ref):
    x = x_ref[...]
    g = jnp.dot(x, wg_ref[...], preferred_element_type=F32)
    u = jnp.dot(x, wu_ref[...], preferred_element_type=F32)
    o_ref[...] = (g * jax.nn.sigmoid(g) * u).astype(o_ref.dtype)


def swiglu_hidden(x, wg, wu, layer, cast=None):
    m, k = x.shape
    n = wg.shape[-1]
    tm, tn = _matmul_tiles(m, k, n, 2, n_weights=2)
    return _grid_call(
        _swiglu_kernel, (x, wg, wu),
        grid=(m // tm, n // tn),
        in_specs=[pl.BlockSpec((tm, k), lambda i, j: (i, 0)),
                  pl.BlockSpec((None, k, tn), lambda i, j: (layer, 0, j)),
                  pl.BlockSpec((None, k, tn), lambda i, j: (layer, 0, j))],
        out_spec=pl.BlockSpec((tm, tn), lambda i, j: (i, j)),
        out_shape=jax.ShapeDtypeStruct((m, n), BF16),
        name="swiglu_hidden", cast=cast)


def _norm_residual_kernel(y_ref, h_ref, gpost_ref, gnext_ref, h_out_ref, xn_out_ref):
    h = h_ref[...] + _rms(y_ref[...], gpost_ref[...])
    h_out_ref[...] = h
    xn_out_ref[...] = _rms(h, gnext_ref[...]).astype(xn_out_ref.dtype)


def norm_residual(y, h, g_post, g_next, layer, next_layer):
    m, d = h.shape
    tm = _pick(m, (256, 128, 8))
    row = pl.BlockSpec((tm, d), lambda i: (i, 0))
    return pl.pallas_call(
        _norm_residual_kernel,
        grid=(m // tm,),
        in_specs=[row, row,
                  pl.BlockSpec((None, 1, d), lambda i: (layer, 0, 0)),
                  pl.BlockSpec((None, 1, d), lambda i: (next_layer, 0, 0))],
        out_specs=[row, row],
        out_shape=[jax.ShapeDtypeStruct((m, d), F32), jax.ShapeDtypeStruct((m, d), BF16)],
        compiler_params=_params("parallel"),
        name="norm_residual",
    )(y, h, g_post, g_next)


def _sb_scores(q, k, q0, k0):
    z = lax.dot_general(q, k, (((1,), (1,)), ((), ())), preferred_element_type=F32) * HEAD_DIM ** -0.5
    row = lax.broadcasted_iota(jnp.int32, z.shape, 0)
    col = lax.broadcasted_iota(jnp.int32, z.shape, 1)
    causal = (col - row) < (q0 - k0)
    log_beta = jnp.minimum(z, 0.0) - jnp.log(1.0 + jnp.exp(-jnp.abs(z)))
    log_stay = jnp.where(causal, log_beta - z, 0.0)
    return log_beta, log_stay, causal


def _suffix_sums(tiles, tri2):
    t = ATT_TILE
    x = jnp.concatenate(tiles, axis=0) if len(tiles) > 1 else tiles[0]
    hi = x.astype(BF16)
    lo = (x - hi.astype(F32)).astype(BF16)
    sums = jnp.dot(jnp.concatenate([hi, lo], axis=1), tri2, preferred_element_type=F32)
    return [(sums[n * t:(n + 1) * t, :t], sums[n * t:(n + 1) * t, t:]) for n in range(len(tiles))]


def _sb_attention_kernel(q_ref, k_ref, v_ref, tri2_ref, o_ref):
    s_len = q_ref.shape[0]
    t = ATT_TILE
    nw = SB_WINDOW_TILES
    wk = nw * t

    def windows(tiles):
        pre = []
        for i in tiles:
            q0 = pl.multiple_of(i * t, t)
            first = jnp.maximum(i - (nw - 1), 0)
            k0 = pl.multiple_of(first * t, t)
            q = q_ref[pl.ds(q0, t), :]
            pre.append((q, q0, first, k0) + _sb_scores(q, k_ref[pl.ds(k0, wk), :], q0, k0))
        sums = []
        for c in range(0, len(pre), SB_Q_TILES_PER_SUFFIX_DOT):
            part = pre[c:c + SB_Q_TILES_PER_SUFFIX_DOT]
            sums += _suffix_sums([p[5][:, d * t:(d + 1) * t] for p in part for d in range(nw)], tri2_ref[...])
        out = []
        for u, (q, q0, first, k0, log_beta, _, causal) in enumerate(pre):
            after, run = [], jnp.zeros((t, t), F32)
            for d in reversed(range(nw)):
                cum, tot = sums[u * nw + d]
                after.append(cum + run)
                run = run + tot
            after = jnp.concatenate(after[::-1], axis=1)
            w = jnp.where(causal, jnp.exp(log_beta + after), 0.0)
            acc = jnp.dot(w.astype(BF16), v_ref[pl.ds(k0, wk), :], preferred_element_type=F32)
            out.append((q, q0, first - 1, run, acc))
        return out

    def more(j, run):
        return jnp.logical_and(j >= 0, jnp.max(run) > -F32_EXP_UNDERFLOW)

    def tail(q, q0, j, run, acc):
        def k_tile(state):
            j, _, run, acc = state
            k0 = pl.multiple_of(j * t, t)
            log_beta, log_stay, _ = _sb_scores(q, k_ref[pl.ds(k0, t), :], q0, k0)
            (cum, tot), = _suffix_sums([log_stay], tri2_ref[...])
            w = jnp.exp(log_beta + run + cum)
            acc = acc + jnp.dot(w.astype(BF16), v_ref[pl.ds(k0, t), :], preferred_element_type=F32)
            run = run + tot
            return j - 1, more(j - 1, run), run, acc

        return lax.while_loop(lambda st: st[1], k_tile, (j, more(j, run), run, acc))[3]

    def q_group(g, carry):
        wins = windows([g * SB_Q_TILES_PER_STEP + u for u in range(SB_Q_TILES_PER_STEP)])
        for q, q0, j, run, acc in wins:
            o_ref[pl.ds(q0, t), :] = tail(q, q0, j, run, acc).astype(o_ref.dtype)
        return carry

    lax.fori_loop(0, s_len // (t * SB_Q_TILES_PER_STEP), q_group, 0)


def _suffix_sum_matrix():
    r = jnp.arange(ATT_TILE)
    strict_lower = (r[:, None] > r[None, :]).astype(BF16)
    half = jnp.concatenate([strict_lower, jnp.ones((ATT_TILE, ATT_TILE), BF16)], axis=1)
    return jnp.concatenate([half, half], axis=0)


def sb_attention(qkv, batch, seq):
    m = qkv.shape[0]
    h = N_HEADS_SB
    assert seq % (ATT_TILE * SB_Q_TILES_PER_STEP) == 0 and seq >= SB_WINDOW_TILES * ATT_TILE
    blk = lambda off: pl.BlockSpec((seq, HEAD_DIM), lambda b, n: (b, n + off))
    tri2 = _suffix_sum_matrix()
    return pl.pallas_call(
        _sb_attention_kernel,
        grid=(batch, h),
        in_specs=[blk(0), blk(h), blk(2 * h), pl.BlockSpec(tri2.shape, lambda b, n: (0, 0))],
        out_specs=blk(0),
        out_shape=jax.ShapeDtypeStruct((m, W_SB), BF16),
        compiler_params=_params("parallel", "parallel"),
        name="sb_attention",
    )(qkv, qkv, qkv, tri2)


def _dil_attention_kernel(q_ref, kp_ref, kc_ref, vp_ref, vc_ref, o_ref, lse_ref):
    t = ATT_TILE
    i = pl.program_id(1)
    in_band, own_tile, dist = _dil_band(1)
    lane = lax.broadcasted_iota(jnp.int32, (t, HEAD_DIM), 1)
    for a in range(q_ref.shape[0] // t):
        rows = slice(a * t, (a + 1) * t)
        prev = slice((a - 1) * t, a * t)
        valid = in_band if a else jnp.logical_and(in_band, jnp.logical_or(own_tile, i > 0))
        lse = jnp.zeros((t, HEAD_DIM), F32)
        for hh in range(N_HEADS_PER_DIL):
            sl = slice(hh * HEAD_DIM, (hh + 1) * HEAD_DIM)
            k = jnp.concatenate([kc_ref[prev, sl] if a else kp_ref[:, sl], kc_ref[rows, sl]], axis=0)
            v = jnp.concatenate([vc_ref[prev, sl] if a else vp_ref[:, sl], vc_ref[rows, sl]], axis=0)
            out, lse_h = _dil_tile(q_ref[rows, sl], k, v, valid, dist, hh)
            o_ref[rows, sl] = out.astype(o_ref.dtype)
            lse = jnp.where(lane == hh, lse_h, lse)
        lse_ref[rows, :] = lse


def _dil_tile(q, k, v, valid, dist, head):
    slope = 2.0 ** (-8.0 * (head + 1) / N_HEADS_PER_DIL)
    s = lax.dot_general(q, k, (((1,), (1,)), ((), ())), preferred_element_type=F32) * HEAD_DIM ** -0.5
    s = jnp.where(valid, s - slope * dist, -jnp.inf)
    mx = jnp.max(s, axis=1, keepdims=True)
    e = jnp.exp(s - mx).astype(BF16)
    v_ones = jnp.concatenate([v, jnp.ones_like(v)], axis=1)
    pv_den = jnp.dot(e, v_ones, preferred_element_type=F32)
    pv, den = pv_den[:, :HEAD_DIM], pv_den[:, HEAD_DIM:]
    return pv / den, mx + jnp.log(den[:, :1])


def _dil_band(dilation):
    t = ATT_TILE
    row = lax.broadcasted_iota(jnp.int32, (t, 2 * t), 0)
    col = lax.broadcasted_iota(jnp.int32, (t, 2 * t), 1)
    steps = t + row - col
    return jnp.logical_and(steps >= 0, steps <= t), col >= t, (steps * dilation).astype(F32)


def _dil_strided_kernel(q_ref, kp_ref, kc_ref, vp_ref, vc_ref, o_ref, lse_ref,
                        qw, kpw, kcw, vpw, vcw, ow, lse_rows, *, dilation):
    t = ATT_TILE
    r = dilation
    half = r // 2
    heads = N_HEADS_PER_DIL
    i = pl.program_id(1)
    for src, words in ((q_ref, qw), (kp_ref, kpw), (kc_ref, kcw), (vp_ref, vpw), (vc_ref, vcw)):
        for hh in range(heads):
            words[hh] = pltpu.bitcast(src[:, hh * HEAD_DIM:(hh + 1) * HEAD_DIM], jnp.uint32)
    in_band, own_tile, dist = _dil_band(r)
    valid = jnp.logical_and(in_band, jnp.logical_or(own_tile, i > 0))
    lane = lax.broadcasted_iota(jnp.int32, (t, HEAD_DIM), 1)
    high = jnp.uint32(0xFFFF0000)

    def unpack(words, odd):
        return pltpu.bitcast(words & high if odd else words << 16, F32).astype(BF16)

    def bf16_bits(x):
        return pltpu.bitcast(x.astype(BF16).astype(F32), jnp.uint32)

    def class_pair(p, carry):
        sel = pl.ds(p, t, stride=half)
        lses = [jnp.zeros((t, HEAD_DIM), F32)] * 2
        for hh in range(heads):
            wq, wkp, wkc, wvp, wvc = (w[hh, sel, :] for w in (qw, kpw, kcw, vpw, vcw))
            outs = []
            for odd in (0, 1):
                k = jnp.concatenate([unpack(wkp, odd), unpack(wkc, odd)], axis=0)
                v = jnp.concatenate([unpack(wvp, odd), unpack(wvc, odd)], axis=0)
                out, lse_h = _dil_tile(unpack(wq, odd), k, v, valid, dist, hh)
                outs.append(out)
                lses[odd] = jnp.where(lane == hh, lse_h, lses[odd])
            ow[hh, sel, :] = (bf16_bits(outs[0]) >> 16) | (bf16_bits(outs[1]) & high)
        for odd in (0, 1):
            lse_rows[pl.ds(2 * p + odd, t, stride=r), :] = lses[odd]
        return carry

    lax.fori_loop(0, half, class_pair, 0)
    for hh in range(heads):
        o_ref[:, hh * HEAD_DIM:(hh + 1) * HEAD_DIM] = pltpu.bitcast(ow[hh], BF16)
    lse_ref[...] = lse_rows[...]


def dil_attention_group(qkv, batch, seq, group):
    m, wq = qkv.shape
    _, r = DIL_CONFIGS[group]
    t = ATT_TILE
    w = W_DIL_GROUP
    assert wq % w == 0 and seq % (t * r) == 0
    q_col = (3 * W_SB) // w + group
    k_col = q_col + W_DIL // w
    v_col = k_col + W_DIL // w
    out_shape = [jax.ShapeDtypeStruct((m, w), BF16), jax.ShapeDtypeStruct((m, HEAD_DIM), F32)]

    if r > 1:
        assert r % 2 == 0
        rows = t * r
        nb = seq // rows
        cur = lambda col: pl.BlockSpec((rows, w), lambda b, i: (b * nb + i, col))
        prev = lambda col: pl.BlockSpec((rows, w), lambda b, i: (b * nb + jnp.maximum(i - 1, 0), col))
        words = pltpu.VMEM((N_HEADS_PER_DIL, rows // 2, HEAD_DIM), jnp.uint32)
        return pl.pallas_call(
            functools.partial(_dil_strided_kernel, dilation=r),
            grid=(batch, nb),
            in_specs=[cur(q_col), prev(k_col), cur(k_col), prev(v_col), cur(v_col)],
            out_specs=[pl.BlockSpec((rows, w), lambda b, i: (b * nb + i, 0)),
                       pl.BlockSpec((rows, HEAD_DIM), lambda b, i: (b * nb + i, 0))],
            out_shape=out_shape,
            scratch_shapes=[words] * 6 + [pltpu.VMEM((rows, HEAD_DIM), F32)],
            compiler_params=_params("parallel", "arbitrary"),
            name=f"dil_attention_r{r}",
        )(qkv, qkv, qkv, qkv, qkv)

    nb = seq // t
    tiles = _pick(nb, (DIL_TILES_PER_STEP, 1))
    ns = nb // tiles

    def spec(col, prev):
        if prev:
            return pl.BlockSpec((t, w), lambda b, i: (b * nb + jnp.maximum(i * tiles - 1, 0), col))
        return pl.BlockSpec((tiles * t, w), lambda b, i: (b * ns + i, col))

    out_spec = lambda width: pl.BlockSpec((tiles * t, width), lambda b, i: (b * ns + i, 0))
    return pl.pallas_call(
        _dil_attention_kernel,
        grid=(batch, ns),
        in_specs=[spec(q_col, False), spec(k_col, True), spec(k_col, False),
                  spec(v_col, True), spec(v_col, False)],
        out_specs=[out_spec(w), out_spec(HEAD_DIM)],
        out_shape=out_shape,
        compiler_params=_params("parallel", "arbitrary"),
        name="dil_attention_r1",
    )(qkv, qkv, qkv, qkv, qkv)


def _dil_merge_kernel(*refs):
    n = len(DIL_CONFIGS)
    o_refs, lse_refs, out_ref = refs[:n], refs[n:2 * n], refs[2 * n]
    lses = [r[...] for r in lse_refs]
    mx = functools.reduce(jnp.maximum, lses)
    ws = [jnp.exp(l - mx) for l in lses]
    total = sum(ws)
    alphas = [w / total for w in ws]
    for hh in range(N_HEADS_PER_DIL):
        sl = slice(hh * HEAD_DIM, (hh + 1) * HEAD_DIM)
        merged = sum(a[:, hh:hh + 1] * o[:, sl] for a, o in zip(alphas, o_refs))
        out_ref[:, sl] = merged.astype(out_ref.dtype)


def dil_merge(outs, lses):
    m, w = outs[0].shape
    tm = _pick(m, (1024, 512, 256, 128, 8))
    row = pl.BlockSpec((tm, w), lambda i: (i, 0))
    lse_row = pl.BlockSpec((tm, HEAD_DIM), lambda i: (i, 0))
    return pl.pallas_call(
        _dil_merge_kernel,
        grid=(m // tm,),
        in_specs=[row] * len(outs) + [lse_row] * len(lses),
        out_specs=row,
        out_shape=jax.ShapeDtypeStruct((m, w), BF16),
        compiler_params=_params("parallel"),
        name="dil_merge",
    )(*outs, *lses)


def _gated_proj_kernel(x_ref, wgs_ref, wgd_ref, osb_ref, od_ref, psb_ref, pd_ref, o_ref):
    x = x_ref[...]
    gate_sb = jax.nn.sigmoid(jnp.dot(x, wgs_ref[...], preferred_element_type=F32))
    gate_d = jax.nn.sigmoid(jnp.dot(x, wgd_ref[...], preferred_element_type=F32))
    a = jnp.dot(osb_ref[...], psb_ref[...], preferred_element_type=F32)
    b = jnp.dot(od_ref[...], pd_ref[...], preferred_element_type=F32)
    o_ref[...] = (gate_sb * a + gate_d * b).astype(o_ref.dtype)


def gated_proj(xn, w_in, w_layer, o_sb, o_d, p_sb, p_d, layer, cast=None):
    m, k = xn.shape
    d = p_sb.shape[-1]
    col_sb = w_in.shape[-1] - 2 * d
    tm = _pick(m, (1024, 512, 256, 128))
    tn = next(c for c in (512, 256, 128) if d % c == 0 and col_sb % c == 0)
    c_sb, c_d = col_sb // tn, (col_sb + d) // tn
    return _grid_call(
        _gated_proj_kernel, (xn, w_in, w_in, o_sb, o_d, p_sb, p_d),
        grid=(m // tm, d // tn),
        in_specs=[pl.BlockSpec((tm, k), lambda i, j: (i, 0)),
                  pl.BlockSpec((None, k, tn), lambda i, j: (w_layer, 0, j + c_sb)),
                  pl.BlockSpec((None, k, tn), lambda i, j: (w_layer, 0, j + c_d)),
                  pl.BlockSpec((tm, o_sb.shape[1]), lambda i, j: (i, 0)),
                  pl.BlockSpec((tm, o_d.shape[1]), lambda i, j: (i, 0)),
                  pl.BlockSpec((None, p_sb.shape[1], tn), lambda i, j: (layer, 0, j)),
                  pl.BlockSpec((None, p_d.shape[1], tn), lambda i, j: (layer, 0, j))],
        out_spec=pl.BlockSpec((tm, tn), lambda i, j: (i, j)),
        out_shape=jax.ShapeDtypeStruct((m, d), BF16),
        name="gated_proj", cast=cast)


def _ffn_close_ple_kernel(y_ref, p_ref, h_ref, gd_ref, gu_ref, wp_ref,
                          g_ffn_post_ref, g_gate_ref, g_post_ref, g_next_ref, h_out_ref, xn_out_ref):
    h = h_ref[...] + _rms(y_ref[...], g_ffn_post_ref[...])
    xn = _rms(h, g_gate_ref[...]).astype(BF16)
    low = jnp.dot(xn, gd_ref[...], preferred_element_type=F32)
    gate = jax.nn.sigmoid(jnp.dot(low.astype(BF16), gu_ref[...], preferred_element_type=F32))
    e = jnp.dot(p_ref[...].astype(BF16), wp_ref[...], preferred_element_type=F32) * gate
    h = h + _rms(e, g_post_ref[...])
    h_out_ref[...] = h
    xn_out_ref[...] = _rms(h, g_next_ref[...]).astype(xn_out_ref.dtype)


def ffn_close_ple_block(y, p, h, g_down, g_up, w_ple, g_ffn_post, g_gate, g_post, g_next,
                        layer, next_layer):
    m, d = h.shape
    dp = p.shape[-1]
    tm = _pick(m, (256, 128, 8))
    row = pl.BlockSpec((tm, d), lambda i: (i, 0))
    gain = lambda l: pl.BlockSpec((None, 1, d), lambda i: (l, 0, 0))
    return pl.pallas_call(
        _ffn_close_ple_kernel,
        grid=(m // tm,),
        in_specs=[row,
                  pl.BlockSpec((None, tm, dp), lambda i: (layer, i, 0)),
                  row,
                  pl.BlockSpec((None, d, dp), lambda i: (layer, 0, 0)),
                  pl.BlockSpec((None, dp, d), lambda i: (layer, 0, 0)),
                  pl.BlockSpec((None, dp, d), lambda i: (layer, 0, 0)),
                  gain(layer), gain(layer), gain(layer), gain(next_layer)],
        out_specs=[row, row],
        out_shape=[jax.ShapeDtypeStruct((m, d), F32), jax.ShapeDtypeStruct((m, d), BF16)],
        compiler_params=_params("parallel"),
        name="ffn_close_ple",
    )(y, p, h, g_down, g_up, w_ple, g_ffn_post, g_gate, g_post, g_next)


def kernel(x, p, w_in, w_proj_sb, w_proj_dil, w_out, g_mix_pre, g_mix_post,
           w_ffn_gate, w_ffn_up, w_ffn_down, g_ffn_pre, g_ffn_post,
           w_ple_in, w_ple_gate_down, w_ple_gate_up, g_ple_gate, g_ple_post):
    b, s, d = x.shape
    depth = w_in.shape[0]
    m = b * s
    assert w_in.shape[-1] == W_QKV + 2 * d

    bf = lambda w: w.astype(BF16)
    w_in_bf = cast_layer_bf16(w_in, 0)
    w_proj_sb, w_proj_dil = bf(w_proj_sb), bf(w_proj_dil)
    w_ple_in, w_ple_gate_down, w_ple_gate_up = bf(w_ple_in), bf(w_ple_gate_down), bf(w_ple_gate_up)
    gains = lambda g: g.reshape(depth, 1, d)
    g_mix_pre, g_mix_post, g_ffn_pre, g_ffn_post, g_ple_gate, g_ple_post = map(
        gains, (g_mix_pre, g_mix_post, g_ffn_pre, g_ffn_post, g_ple_gate, g_ple_post))
    p = p.reshape(depth, m, p.shape[-1])

    h = x.reshape(m, d)
    xn = rmsnorm_bf16(h, g_mix_pre, 0)
    for i in range(depth):
        qkv, w_out_bf = matmul(xn, w_in_bf, 0, col0=0, n_out=W_QKV, out_dtype=BF16,
                               name="in_proj_qkv", cast=(w_out, i))
        o_sb = sb_attention(qkv, b, s)
        groups = [dil_attention_group(qkv, b, s, g) for g in range(len(DIL_CONFIGS))]
        o_d = dil_merge([o for o, _ in groups], [l for _, l in groups])
        merged, w_gate_bf = gated_proj(xn, w_in_bf, 0, o_sb, o_d, w_proj_sb, w_proj_dil, i,
                                       cast=(w_ffn_gate, i))
        y, w_up_bf = matmul(merged, w_out_bf, 0, col0=0, n_out=d, out_dtype=F32, name="out_proj",
                            cast=(w_ffn_up, i))
        h, xn = norm_residual(y, h, g_mix_post, g_ffn_pre, i, i)
        hidden, w_down_bf = swiglu_hidden(xn, w_gate_bf, w_up_bf, 0, cast=(w_ffn_down, i))
        if i + 1 < depth:
            y, w_in_bf = matmul(hidden, w_down_bf, 0, col0=0, n_out=d, out_dtype=F32,
                                name="ffn_down", cast=(w_in, i + 1))
        else:
            y = matmul(hidden, w_down_bf, 0, col0=0, n_out=d, out_dtype=F32, name="ffn_down")
        h, xn = ffn_close_ple_block(y, p, h, w_ple_gate_down, w_ple_gate_up, w_ple_in,
                                    g_ffn_post, g_ple_gate, g_ple_post, g_mix_pre, i, (i + 1) % depth)
    return h.reshape(b, s, d)
```

```python
import functools

import jax
import jax.numpy as jnp
from jax import lax
from jax.experimental import pallas as pl
from jax.experimental.pallas import tpu as pltpu

HEAD_DIM = 128
N_HEADS_SB = 8
DIL_CONFIGS = ((128, 1), (512, 4), (2048, 16))
N_HEADS_PER_DIL = 4
EPS = 1e-6

W_SB = N_HEADS_SB * HEAD_DIM
W_DIL_GROUP = N_HEADS_PER_DIL * HEAD_DIM
W_DIL = W_DIL_GROUP * len(DIL_CONFIGS)
W_QKV = 3 * W_SB + 3 * W_DIL
ATT_TILE = DIL_CONFIGS[0][0] // DIL_CONFIGS[0][1]
assert all(w // r == ATT_TILE for w, r in DIL_CONFIGS)
assert ATT_TILE == HEAD_DIM

F32_EXP_UNDERFLOW = 104.0
SB_WINDOW_TILES = 3
SB_Q_TILES_PER_STEP = 4
SB_Q_TILES_PER_SUFFIX_DOT = 2
DIL_TILES_PER_STEP = 2

MIB = 1024 * 1024
V7X_VMEM_BYTES = 64 * MIB
VMEM_LIMIT_BYTES = V7X_VMEM_BYTES - 6 * MIB
VMEM_LIMIT_MAX_BYTES = V7X_VMEM_BYTES - 2 * MIB
VMEM_TILE_HEADROOM_BYTES = 8 * MIB
VMEM_TILE_HEADROOM_MIN_BYTES = 4 * MIB

BF16 = jnp.bfloat16
F32 = jnp.float32
BF16_SUBLANES = 16


def _params(*sem, vmem_limit=VMEM_LIMIT_BYTES):
    return pltpu.CompilerParams(dimension_semantics=sem, vmem_limit_bytes=vmem_limit)


def _pick(n, candidates):
    for c in candidates:
        if n % c == 0:
            return c
    raise ValueError(f"no tile in {candidates} divides {n}")


def _matmul_tiles(m, k, n, out_bytes, n_weights=1, col0=0):
    for tm, tn in ((1024, 1280), (1024, 1024), (2048, 512), (1024, 768), (2048, 256), (1024, 512),
                   (1024, 256), (512, 512), (512, 256), (256, 256), (256, 128), (128, 128)):
        if m % tm or n % tn or col0 % tn:
            continue
        need = 2 * (tm * k * 2 + n_weights * k * tn * 2 + tm * tn * out_bytes) + n_weights * tm * tn * 4
        if need <= VMEM_LIMIT_BYTES - VMEM_TILE_HEADROOM_BYTES:
            return tm, tn, VMEM_LIMIT_BYTES
        if need <= VMEM_LIMIT_MAX_BYTES - VMEM_TILE_HEADROOM_MIN_BYTES:
            return tm, tn, VMEM_LIMIT_MAX_BYTES
    raise ValueError(f"no matmul tiling for {(m, k, n)}")


def _rms(x, g):
    return x * lax.rsqrt(jnp.mean(x * x, axis=-1, keepdims=True) + EPS) * g


def _rmsnorm_kernel(x_ref, g_ref, o_ref):
    o_ref[...] = _rms(x_ref[...], g_ref[...]).astype(o_ref.dtype)


def rmsnorm_bf16(x, g, layer):
    m, d = x.shape
    tm = _pick(m, (512, 256, 128, 8))
    return pl.pallas_call(
        _rmsnorm_kernel,
        grid=(m // tm,),
        in_specs=[pl.BlockSpec((tm, d), lambda i: (i, 0)),
                  pl.BlockSpec((None, 1, d), lambda i: (layer, 0, 0))],
        out_specs=pl.BlockSpec((tm, d), lambda i: (i, 0)),
        out_shape=jax.ShapeDtypeStruct((m, d), BF16),
        compiler_params=_params("parallel"),
        name="rmsnorm",
    )(x, g)


def _cast_kernel(w_ref, o_ref):
    o_ref[...] = w_ref[...].astype(o_ref.dtype)


def cast_layer_bf16(w, layer):
    _, r, c = w.shape
    rows = _pick(r, (128, 64, 32, 16))
    return pl.pallas_call(
        _cast_kernel,
        grid=(r // rows,),
        in_specs=[pl.BlockSpec((None, rows, c), lambda i: (layer, i, 0))],
        out_specs=pl.BlockSpec((None, rows, c), lambda i: (0, i, 0)),
        out_shape=jax.ShapeDtypeStruct((1, r, c), BF16),
        compiler_params=_params("parallel"),
        name="cast_bf16",
    )(w)


def _cast_rows(total_rows, steps):
    for rows in range(BF16_SUBLANES, total_rows + 1, BF16_SUBLANES):
        if total_rows % rows == 0 and total_rows // rows <= steps:
            return rows
    raise ValueError(f"cannot cover {total_rows} rows in {steps} steps")


def _grid_call(body, args, *, grid, in_specs, out_spec, out_shape, name, cast=None,
               vmem_limit=VMEM_LIMIT_BYTES):
    if cast is None:
        return pl.pallas_call(
            body, grid=grid, in_specs=in_specs, out_specs=out_spec, out_shape=out_shape,
            compiler_params=_params("parallel", "parallel", vmem_limit=vmem_limit), name=name)(*args)
    src, layer = cast
    _, r, c = src.shape
    rows = _cast_rows(r, grid[0] * grid[1])
    blk = lambda i, j: jnp.minimum(i * grid[1] + j, r // rows - 1)
    n_in = len(in_specs)

    def body_with_cast(*refs):
        body(*refs[:n_in], refs[n_in + 1])
        refs[n_in + 2][...] = refs[n_in][...].astype(BF16)

    return pl.pallas_call(
        body_with_cast, grid=grid,
        in_specs=[*in_specs, pl.BlockSpec((None, rows, c), lambda i, j: (layer, blk(i, j), 0))],
        out_specs=[out_spec, pl.BlockSpec((None, rows, c), lambda i, j: (0, blk(i, j), 0))],
        out_shape=[out_shape, jax.ShapeDtypeStruct((1, r, c), BF16)],
        compiler_params=_params("arbitrary", "arbitrary", vmem_limit=vmem_limit), name=name)(*args, src)


def _matmul_kernel(x_ref, w_ref, o_ref):
    o_ref[...] = jnp.dot(x_ref[...], w_ref[...], preferred_element_type=F32).astype(o_ref.dtype)


def matmul(x, w, layer, *, col0, n_out, out_dtype, name, cast=None):
    m, k = x.shape
    tm, tn, vmem_limit = _matmul_tiles(m, k, n_out, jnp.dtype(out_dtype).itemsize, col0=col0)
    c0 = col0 // tn
    return _grid_call(
        _matmul_kernel, (x, w),
        grid=(m // tm, n_out // tn),
        in_specs=[pl.BlockSpec((tm, k), lambda i, j: (i, 0)),
                  pl.BlockSpec((None, k, tn), lambda i, j: (layer, 0, j + c0))],
        out_spec=pl.BlockSpec((tm, tn), lambda i, j: (i, j)),
        out_shape=jax.ShapeDtypeStruct((m, n_out), out_dtype),
        name=name, cast=cast, vmem_limit=vmem_limit)


def _swiglu_kernel(x_ref, wg_ref, wu_ref, o_ref):
    x = x_ref[...]
    g = jnp.dot(x, wg_ref[...], preferred_element_type=F32)
    u = jnp.dot(x, wu_ref[...], preferred_element_type=F32)
    o_ref[...] = (g * jax.nn.sigmoid(g) * u).astype(o_ref.dtype)


def swiglu_hidden(x, wg, wu, layer, cast=None):
    m, k = x.shape
    n = wg.shape[-1]
    tm, tn, vmem_limit = _matmul_tiles(m, k, n, 2, n_weights=2)
    return _grid_call(
        _swiglu_kernel, (x, wg, wu),
        grid=(m // tm, n // tn),
        in_specs=[pl.BlockSpec((tm, k), lambda i, j: (i, 0)),
                  pl.BlockSpec((None, k, tn), lambda i, j: (layer, 0, j)),
                  pl.BlockSpec((None, k, tn), lambda i, j: (layer, 0, j))],
        out_spec=pl.BlockSpec((tm, tn), lambda i, j: (i, j)),
        out_shape=jax.ShapeDtypeStruct((m, n), BF16),
        name="swiglu_hidden", cast=cast, vmem_limit=vmem_limit)


def _norm_residual_kernel(y_ref, h_ref, gpost_ref, gnext_ref, h_out_ref, xn_out_ref):
    h = h_ref[...] + _rms(y_ref[...], gpost_ref[...])
    h_out_ref[...] = h
    xn_out_ref[...] = _rms(h, gnext_ref[...]).astype(xn_out_ref.dtype)


def norm_residual(y, h, g_post, g_next, layer, next_layer):
    m, d = h.shape
    tm = _pick(m, (256, 128, 8))
    row = pl.BlockSpec((tm, d), lambda i: (i, 0))
    return pl.pallas_call(
        _norm_residual_kernel,
        grid=(m // tm,),
        in_specs=[row, row,
                  pl.BlockSpec((None, 1, d), lambda i: (layer, 0, 0)),
                  pl.BlockSpec((None, 1, d), lambda i: (next_layer, 0, 0))],
        out_specs=[row, row],
        out_shape=[jax.ShapeDtypeStruct((m, d), F32), jax.ShapeDtypeStruct((m, d), BF16)],
        compiler_params=_params("parallel"),
        name="norm_residual",
    )(y, h, g_post, g_next)


def _sb_scores(q, k, q0, k0):
    z = lax.dot_general(q, k, (((1,), (1,)), ((), ())), preferred_element_type=F32) * HEAD_DIM ** -0.5
    row = lax.broadcasted_iota(jnp.int32, z.shape, 0)
    col = lax.broadcasted_iota(jnp.int32, z.shape, 1)
    causal = (col - row) < (q0 - k0)
    log_beta = jnp.minimum(z, 0.0) - jnp.log(1.0 + jnp.exp(-jnp.abs(z)))
    log_stay = jnp.where(causal, log_beta - z, 0.0)
    return log_beta, log_stay, causal


def _suffix_sums(tiles, tri2):
    t = ATT_TILE
    x = jnp.concatenate(tiles, axis=0) if len(tiles) > 1 else tiles[0]
    hi = x.astype(BF16)
    lo = (x - hi.astype(F32)).astype(BF16)
    sums = jnp.dot(jnp.concatenate([hi, lo], axis=1), tri2, preferred_element_type=F32)
    return [(sums[n * t:(n + 1) * t, :t], sums[n * t:(n + 1) * t, t:]) for n in range(len(tiles))]


def _sb_attention_kernel(q_ref, k_ref, v_ref, tri2_ref, o_ref):
    s_len = q_ref.shape[0]
    t = ATT_TILE
    nw = SB_WINDOW_TILES
    wk = nw * t

    def windows(tiles):
        pre = []
        for i in tiles:
            q0 = pl.multiple_of(i * t, t)
            first = jnp.maximum(i - (nw - 1), 0)
            k0 = pl.multiple_of(first * t, t)
            q = q_ref[pl.ds(q0, t), :]
            pre.append((q, q0, first, k0) + _sb_scores(q, k_ref[pl.ds(k0, wk), :], q0, k0))
        sums = []
        for c in range(0, len(pre), SB_Q_TILES_PER_SUFFIX_DOT):
            part = pre[c:c + SB_Q_TILES_PER_SUFFIX_DOT]
            sums += _suffix_sums([p[5][:, d * t:(d + 1) * t] for p in part for d in range(nw)], tri2_ref[...])
        out = []
        for u, (q, q0, first, k0, log_beta, _, causal) in enumerate(pre):
            after, run = [], jnp.zeros((t, t), F32)
            for d in reversed(range(nw)):
                cum, tot = sums[u * nw + d]
                after.append(cum + run)
                run = run + tot
            after = jnp.concatenate(after[::-1], axis=1)
            w = jnp.where(causal, jnp.exp(log_beta + after), 0.0)
            acc = jnp.dot(w.astype(BF16), v_ref[pl.ds(k0, wk), :], preferred_element_type=F32)
            out.append((q, q0, first - 1, run, acc))
        return out

    def more(j, run):
        return jnp.logical_and(j >= 0, jnp.max(run) > -F32_EXP_UNDERFLOW)

    def tail(q, q0, j, run, acc):
        def k_tile(state):
            j, _, run, acc = state
            k0 = pl.multiple_of(j * t, t)
            log_beta, log_stay, _ = _sb_scores(q, k_ref[pl.ds(k0, t), :], q0, k0)
            (cum, tot), = _suffix_sums([log_stay], tri2_ref[...])
            w = jnp.exp(log_beta + run + cum)
            acc = acc + jnp.dot(w.astype(BF16), v_ref[pl.ds(k0, t), :], preferred_element_type=F32)
            run = run + tot
            return j - 1, more(j - 1, run), run, acc

        return lax.while_loop(lambda st: st[1], k_tile, (j, more(j, run), run, acc))[3]

    def q_group(g, carry):
        wins = windows([g * SB_Q_TILES_PER_STEP + u for u in range(SB_Q_TILES_PER_STEP)])
        for q, q0, j, run, acc in wins:
            o_ref[pl.ds(q0, t), :] = tail(q, q0, j, run, acc).astype(o_ref.dtype)
        return carry

    lax.fori_loop(0, s_len // (t * SB_Q_TILES_PER_STEP), q_group, 0)


def _suffix_sum_matrix():
    r = jnp.arange(ATT_TILE)
    strict_lower = (r[:, None] > r[None, :]).astype(BF16)
    half = jnp.concatenate([strict_lower, jnp.ones((ATT_TILE, ATT_TILE), BF16)], axis=1)
    return jnp.concatenate([half, half], axis=0)


def sb_attention(qkv, batch, seq):
    m = qkv.shape[0]
    h = N_HEADS_SB
    assert seq % (ATT_TILE * SB_Q_TILES_PER_STEP) == 0 and seq >= SB_WINDOW_TILES * ATT_TILE
    blk = lambda off: pl.BlockSpec((seq, HEAD_DIM), lambda b, n: (b, n + off))
    tri2 = _suffix_sum_matrix()
    return pl.pallas_call(
        _sb_attention_kernel,
        grid=(batch, h),
        in_specs=[blk(0), blk(h), blk(2 * h), pl.BlockSpec(tri2.shape, lambda b, n: (0, 0))],
        out_specs=blk(0),
        out_shape=jax.ShapeDtypeStruct((m, W_SB), BF16),
        compiler_params=_params("parallel", "parallel"),
        name="sb_attention",
    )(qkv, qkv, qkv, tri2)


def _dil_attention_kernel(q_ref, kp_ref, kc_ref, vp_ref, vc_ref, o_ref, lse_ref):
    t = ATT_TILE
    i = pl.program_id(1)
    in_band, own_tile, dist = _dil_band(1)
    lane = lax.broadcasted_iota(jnp.int32, (t, HEAD_DIM), 1)
    for a in range(q_ref.shape[0] // t):
        rows = slice(a * t, (a + 1) * t)
        prev = slice((a - 1) * t, a * t)
        valid = in_band if a else jnp.logical_and(in_band, jnp.logical_or(own_tile, i > 0))
        lse = jnp.zeros((t, HEAD_DIM), F32)
        for hh in range(N_HEADS_PER_DIL):
            sl = slice(hh * HEAD_DIM, (hh + 1) * HEAD_DIM)
            k = jnp.concatenate([kc_ref[prev, sl] if a else kp_ref[:, sl], kc_ref[rows, sl]], axis=0)
            v = jnp.concatenate([vc_ref[prev, sl] if a else vp_ref[:, sl], vc_ref[rows, sl]], axis=0)
            out, lse_h = _dil_tile(q_ref[rows, sl], k, v, valid, dist, hh)
            o_ref[rows, sl] = out.astype(o_ref.dtype)
            lse = jnp.where(lane == hh, lse_h, lse)
        lse_ref[rows, :] = lse


def _dil_tile(q, k, v, valid, dist, head):
    slope = 2.0 ** (-8.0 * (head + 1) / N_HEADS_PER_DIL)
    s = lax.dot_general(q, k, (((1,), (1,)), ((), ())), preferred_element_type=F32) * HEAD_DIM ** -0.5
    s = jnp.where(valid, s - slope * dist, -jnp.inf)
    mx = jnp.max(s, axis=1, keepdims=True)
    e = jnp.exp(s - mx).astype(BF16)
    v_ones = jnp.concatenate([v, jnp.ones_like(v)], axis=1)
    pv_den = jnp.dot(e, v_ones, preferred_element_type=F32)
    pv, den = pv_den[:, :HEAD_DIM], pv_den[:, HEAD_DIM:]
    return pv / den, mx + jnp.log(den[:, :1])


def _dil_band(dilation):
    t = ATT_TILE
    row = lax.broadcasted_iota(jnp.int32, (t, 2 * t), 0)
    col = lax.broadcasted_iota(jnp.int32, (t, 2 * t), 1)
    steps = t + row - col
    return jnp.logical_and(steps >= 0, steps <= t), col >= t, (steps * dilation).astype(F32)


def _dil_strided_kernel(q_ref, kp_ref, kc_ref, vp_ref, vc_ref, o_ref, lse_ref,
                        qw, kpw, kcw, vpw, vcw, ow, lse_rows, *, dilation):
    t = ATT_TILE
    r = dilation
    half = r // 2
    heads = N_HEADS_PER_DIL
    i = pl.program_id(1)
    for src, words in ((q_ref, qw), (kp_ref, kpw), (kc_ref, kcw), (vp_ref, vpw), (vc_ref, vcw)):
        for hh in range(heads):
            words[hh] = pltpu.bitcast(src[:, hh * HEAD_DIM:(hh + 1) * HEAD_DIM], jnp.uint32)
    in_band, own_tile, dist = _dil_band(r)
    valid = jnp.logical_and(in_band, jnp.logical_or(own_tile, i > 0))
    lane = lax.broadcasted_iota(jnp.int32, (t, HEAD_DIM), 1)
    high = jnp.uint32(0xFFFF0000)

    def unpack(words, odd):
        return pltpu.bitcast(words & high if odd else words << 16, F32).astype(BF16)

    def bf16_bits(x):
        return pltpu.bitcast(x.astype(BF16).astype(F32), jnp.uint32)

    def class_pair(p, carry):
        sel = pl.ds(p, t, stride=half)
        lses = [jnp.zeros((t, HEAD_DIM), F32)] * 2
        for hh in range(heads):
            wq, wkp, wkc, wvp, wvc = (w[hh, sel, :] for w in (qw, kpw, kcw, vpw, vcw))
            outs = []
            for odd in (0, 1):
                k = jnp.concatenate([unpack(wkp, odd), unpack(wkc, odd)], axis=0)
                v = jnp.concatenate([unpack(wvp, odd), unpack(wvc, odd)], axis=0)
                out, lse_h = _dil_tile(unpack(wq, odd), k, v, valid, dist, hh)
                outs.append(out)
                lses[odd] = jnp.where(lane == hh, lse_h, lses[odd])
            ow[hh, sel, :] = (bf16_bits(outs[0]) >> 16) | (bf16_bits(outs[1]) & high)
        for odd in (0, 1):
            lse_rows[pl.ds(2 * p + odd, t, stride=r), :] = lses[odd]
        return carry

    lax.fori_loop(0, half, class_pair, 0)
    for hh in range(heads):
        o_ref[:, hh * HEAD_DIM:(hh + 1) * HEAD_DIM] = pltpu.bitcast(ow[hh], BF16)
    lse_ref[...] = lse_rows[...]


def dil_attention_group(qkv, batch, seq, group):
    m, wq = qkv.shape
    _, r = DIL_CONFIGS[group]
    t = ATT_TILE
    w = W_DIL_GROUP
    assert wq % w == 0 and seq % (t * r) == 0
    q_col = (3 * W_SB) // w + group
    k_col = q_col + W_DIL // w
    v_col = k_col + W_DIL // w
    out_shape = [jax.ShapeDtypeStruct((m, w), BF16), jax.ShapeDtypeStruct((m, HEAD_DIM), F32)]

    if r > 1:
        assert r % 2 == 0
        rows = t * r
        nb = seq // rows
        cur = lambda col: pl.BlockSpec((rows, w), lambda b, i: (b * nb + i, col))
        prev = lambda col: pl.BlockSpec((rows, w), lambda b, i: (b * nb + jnp.maximum(i - 1, 0), col))
        words = pltpu.VMEM((N_HEADS_PER_DIL, rows // 2, HEAD_DIM), jnp.uint32)
        return pl.pallas_call(
            functools.partial(_dil_strided_kernel, dilation=r),
            grid=(batch, nb),
            in_specs=[cur(q_col), prev(k_col), cur(k_col), prev(v_col), cur(v_col)],
            out_specs=[pl.BlockSpec((rows, w), lambda b, i: (b * nb + i, 0)),
                       pl.BlockSpec((rows, HEAD_DIM), lambda b, i: (b * nb + i, 0))],
            out_shape=out_shape,
            scratch_shapes=[words] * 6 + [pltpu.VMEM((rows, HEAD_DIM), F32)],
            compiler_params=_params("parallel", "arbitrary"),
            name=f"dil_attention_r{r}",
        )(qkv, qkv, qkv, qkv, qkv)

    nb = seq // t
    tiles = _pick(nb, (DIL_TILES_PER_STEP, 1))
    ns = nb // tiles

    def spec(col, prev):
        if prev:
            return pl.BlockSpec((t, w), lambda b, i: (b * nb + jnp.maximum(i * tiles - 1, 0), col))
        return pl.BlockSpec((tiles * t, w), lambda b, i: (b * ns + i, col))

    out_spec = lambda width: pl.BlockSpec((tiles * t, width), lambda b, i: (b * ns + i, 0))
    return pl.pallas_call(
        _dil_attention_kernel,
        grid=(batch, ns),
        in_specs=[spec(q_col, False), spec(k_col, True), spec(k_col, False),
                  spec(v_col, True), spec(v_col, False)],
        out_specs=[out_spec(w), out_spec(HEAD_DIM)],
        out_shape=out_shape,
        compiler_params=_params("parallel", "arbitrary"),
        name="dil_attention_r1",
    )(qkv, qkv, qkv, qkv, qkv)


def _dil_merge_kernel(*refs):
    n = len(DIL_CONFIGS)
    o_refs, lse_refs, out_ref = refs[:n], refs[n:2 * n], refs[2 * n]
    lses = [r[...] for r in lse_refs]
    mx = functools.reduce(jnp.maximum, lses)
    ws = [jnp.exp(l - mx) for l in lses]
    total = sum(ws)
    alphas = [w / total for w in ws]
    for hh in range(N_HEADS_PER_DIL):
        sl = slice(hh * HEAD_DIM, (hh + 1) * HEAD_DIM)
        merged = sum(a[:, hh:hh + 1] * o[:, sl] for a, o in zip(alphas, o_refs))
        out_ref[:, sl] = merged.astype(out_ref.dtype)


def dil_merge(outs, lses):
    m, w = outs[0].shape
    tm = _pick(m, (1024, 512, 256, 128, 8))
    row = pl.BlockSpec((tm, w), lambda i: (i, 0))
    lse_row = pl.BlockSpec((tm, HEAD_DIM), lambda i: (i, 0))
    return pl.pallas_call(
        _dil_merge_kernel,
        grid=(m // tm,),
        in_specs=[row] * len(outs) + [lse_row] * len(lses),
        out_specs=row,
        out_shape=jax.ShapeDtypeStruct((m, w), BF16),
        compiler_params=_params("parallel"),
        name="dil_merge",
    )(*outs, *lses)


def _gated_proj_kernel(x_ref, wgs_ref, wgd_ref, osb_ref, od_ref, psb_ref, pd_ref, o_ref):
    x = x_ref[...]
    gate_sb = jax.nn.sigmoid(jnp.dot(x, wgs_ref[...], preferred_element_type=F32))
    gate_d = jax.nn.sigmoid(jnp.dot(x, wgd_ref[...], preferred_element_type=F32))
    a = jnp.dot(osb_ref[...], psb_ref[...], preferred_element_type=F32)
    b = jnp.dot(od_ref[...], pd_ref[...], preferred_element_type=F32)
    o_ref[...] = (gate_sb * a + gate_d * b).astype(o_ref.dtype)


def gated_proj(xn, w_in, w_layer, o_sb, o_d, p_sb, p_d, layer, cast=None):
    m, k = xn.shape
    d = p_sb.shape[-1]
    col_sb = w_in.shape[-1] - 2 * d
    tm = _pick(m, (1024, 512, 256, 128))
    tn = next(c for c in (512, 256, 128) if d % c == 0 and col_sb % c == 0)
    c_sb, c_d = col_sb // tn, (col_sb + d) // tn
    return _grid_call(
        _gated_proj_kernel, (xn, w_in, w_in, o_sb, o_d, p_sb, p_d),
        grid=(m // tm, d // tn),
        in_specs=[pl.BlockSpec((tm, k), lambda i, j: (i, 0)),
                  pl.BlockSpec((None, k, tn), lambda i, j: (w_layer, 0, j + c_sb)),
                  pl.BlockSpec((None, k, tn), lambda i, j: (w_layer, 0, j + c_d)),
                  pl.BlockSpec((tm, o_sb.shape[1]), lambda i, j: (i, 0)),
                  pl.BlockSpec((tm, o_d.shape[1]), lambda i, j: (i, 0)),
                  pl.BlockSpec((None, p_sb.shape[1], tn), lambda i, j: (layer, 0, j)),
                  pl.BlockSpec((None, p_d.shape[1], tn), lambda i, j: (layer, 0, j))],
        out_spec=pl.BlockSpec((tm, tn), lambda i, j: (i, j)),
        out_shape=jax.ShapeDtypeStruct((m, d), BF16),
        name="gated_proj", cast=cast)


def _ffn_close_ple_kernel(y_ref, p_ref, h_ref, gd_ref, gu_ref, wp_ref,
                          g_ffn_post_ref, g_gate_ref, g_post_ref, g_next_ref, h_out_ref, xn_out_ref):
    h = h_ref[...] + _rms(y_ref[...], g_ffn_post_ref[...])
    xn = _rms(h, g_gate_ref[...]).astype(BF16)
    low = jnp.dot(xn, gd_ref[...], preferred_element_type=F32)
    gate = jax.nn.sigmoid(jnp.dot(low.astype(BF16), gu_ref[...], preferred_element_type=F32))
    e = jnp.dot(p_ref[...].astype(BF16), wp_ref[...], preferred_element_type=F32) * gate
    h = h + _rms(e, g_post_ref[...])
    h_out_ref[...] = h
    xn_out_ref[...] = _rms(h, g_next_ref[...]).astype(xn_out_ref.dtype)


def ffn_close_ple_block(y, p, h, g_down, g_up, w_ple, g_ffn_post, g_gate, g_post, g_next,
                        layer, next_layer):
    m, d = h.shape
    dp = p.shape[-1]
    tm = _pick(m, (256, 128, 8))
    row = pl.BlockSpec((tm, d), lambda i: (i, 0))
    gain = lambda l: pl.BlockSpec((None, 1, d), lambda i: (l, 0, 0))
    return pl.pallas_call(
        _ffn_close_ple_kernel,
        grid=(m // tm,),
        in_specs=[row,
                  pl.BlockSpec((None, tm, dp), lambda i: (layer, i, 0)),
                  row,
                  pl.BlockSpec((None, d, dp), lambda i: (layer, 0, 0)),
                  pl.BlockSpec((None, dp, d), lambda i: (layer, 0, 0)),
                  pl.BlockSpec((None, dp, d), lambda i: (layer, 0, 0)),
                  gain(layer), gain(layer), gain(layer), gain(next_layer)],
        out_specs=[row, row],
        out_shape=[jax.ShapeDtypeStruct((m, d), F32), jax.ShapeDtypeStruct((m, d), BF16)],
        compiler_params=_params("parallel"),
        name="ffn_close_ple",
    )(y, p, h, g_down, g_up, w_ple, g_ffn_post, g_gate, g_post, g_next)


def kernel(x, p, w_in, w_proj_sb, w_proj_dil, w_out, g_mix_pre, g_mix_post,
           w_ffn_gate, w_ffn_up, w_ffn_down, g_ffn_pre, g_ffn_post,
           w_ple_in, w_ple_gate_down, w_ple_gate_up, g_ple_gate, g_ple_post):
    b, s, d = x.shape
    depth = w_in.shape[0]
    m = b * s
    assert w_in.shape[-1] == W_QKV + 2 * d

    bf = lambda w: w.astype(BF16)
    w_in_bf = cast_layer_bf16(w_in, 0)
    w_proj_sb, w_proj_dil = bf(w_proj_sb), bf(w_proj_dil)
    w_ple_in, w_ple_gate_down, w_ple_gate_up = bf(w_ple_in), bf(w_ple_gate_down), bf(w_ple_gate_up)
    gains = lambda g: g.reshape(depth, 1, d)
    g_mix_pre, g_mix_post, g_ffn_pre, g_ffn_post, g_ple_gate, g_ple_post = map(
        gains, (g_mix_pre, g_mix_post, g_ffn_pre, g_ffn_post, g_ple_gate, g_ple_post))
    p = p.reshape(depth, m, p.shape[-1])

    h = x.reshape(m, d)
    xn = rmsnorm_bf16(h, g_mix_pre, 0)
    for i in range(depth):
        qkv, w_out_bf = matmul(xn, w_in_bf, 0, col0=0, n_out=W_QKV, out_dtype=BF16,
                               name="in_proj_qkv", cast=(w_out, i))
        o_sb = sb_attention(qkv, b, s)
        groups = [dil_attention_group(qkv, b, s, g) for g in range(len(DIL_CONFIGS))]
        o_d = dil_merge([o for o, _ in groups], [l for _, l in groups])
        merged, w_gate_bf = gated_proj(xn, w_in_bf, 0, o_sb, o_d, w_proj_sb, w_proj_dil, i,
                                       cast=(w_ffn_gate, i))
        y, w_up_bf = matmul(merged, w_out_bf, 0, col0=0, n_out=d, out_dtype=F32, name="out_proj",
                            cast=(w_ffn_up, i))
        h, xn = norm_residual(y, h, g_mix_post, g_ffn_pre, i, i)
        hidden, w_down_bf = swiglu_hidden(xn, w_gate_bf, w_up_bf, 0, cast=(w_ffn_down, i))
        if i + 1 < depth:
            y, w_in_bf = matmul(hidden, w_down_bf, 0, col0=0, n_out=d, out_dtype=F32,
                                name="ffn_down", cast=(w_in, i + 1))
        else:
            y = matmul(hidden, w_down_bf, 0, col0=0, n_out=d, out_dtype=F32, name="ffn_down")
        h, xn = ffn_close_ple_block(y, p, h, w_ple_gate_down, w_ple_gate_up, w_ple_in,
                                    g_ffn_post, g_ple_gate, g_ple_post, g_mix_pre, i, (i + 1) % depth)
    return h.reshape(b, s, d)
```

```python
import functools

import jax
import jax.numpy as jnp
from jax import lax
from jax.experimental import pallas as pl
from jax.experimental.pallas import tpu as pltpu

HEAD_DIM = 128
N_HEADS_SB = 8
DIL_CONFIGS = ((128, 1), (512, 4), (2048, 16))
N_HEADS_PER_DIL = 4
EPS = 1e-6

W_SB = N_HEADS_SB * HEAD_DIM
W_DIL_GROUP = N_HEADS_PER_DIL * HEAD_DIM
W_DIL = W_DIL_GROUP * len(DIL_CONFIGS)
W_QKV = 3 * W_SB + 3 * W_DIL
ATT_TILE = DIL_CONFIGS[0][0] // DIL_CONFIGS[0][1]
assert all(w // r == ATT_TILE for w, r in DIL_CONFIGS)
assert ATT_TILE == HEAD_DIM

F32_EXP_UNDERFLOW = 104.0
SB_WINDOW_TILES = 3
SB_Q_TILES_PER_STEP = 4
SB_Q_TILES_PER_SUFFIX_DOT = 2
DIL_TILES_PER_STEP = 2

MIB = 1024 * 1024
V7X_VMEM_BYTES = 64 * MIB
VMEM_LIMIT_BYTES = V7X_VMEM_BYTES - 6 * MIB
VMEM_LIMIT_MAX_BYTES = V7X_VMEM_BYTES - 2 * MIB
VMEM_TILE_HEADROOM_BYTES = 8 * MIB
VMEM_TILE_HEADROOM_MIN_BYTES = 4 * MIB

BF16 = jnp.bfloat16
F32 = jnp.float32
BF16_SUBLANES = 16


def _params(*sem, vmem_limit=VMEM_LIMIT_BYTES):
    return pltpu.CompilerParams(dimension_semantics=sem, vmem_limit_bytes=vmem_limit)


def _pick(n, candidates):
    for c in candidates:
        if n % c == 0:
            return c
    raise ValueError(f"no tile in {candidates} divides {n}")


def _matmul_tiles(m, k, n, out_bytes, n_weights=1, col0=0):
    for tm, tn in ((1024, 1280), (1024, 1024), (2048, 512), (1024, 768), (2048, 256), (1024, 512),
                   (1024, 256), (512, 512), (512, 256), (256, 256), (256, 128), (128, 128)):
        if m % tm or n % tn or col0 % tn:
            continue
        need = 2 * (tm * k * 2 + n_weights * k * tn * 2 + tm * tn * out_bytes) + n_weights * tm * tn * 4
        if need <= VMEM_LIMIT_BYTES - VMEM_TILE_HEADROOM_BYTES:
            return tm, tn, VMEM_LIMIT_BYTES
        if need <= VMEM_LIMIT_MAX_BYTES - VMEM_TILE_HEADROOM_MIN_BYTES:
            return tm, tn, VMEM_LIMIT_MAX_BYTES
    raise ValueError(f"no matmul tiling for {(m, k, n)}")


def _rms(x, g):
    return x * lax.rsqrt(jnp.mean(x * x, axis=-1, keepdims=True) + EPS) * g


def _rmsnorm_kernel(x_ref, g_ref, o_ref):
    o_ref[...] = _rms(x_ref[...], g_ref[...]).astype(o_ref.dtype)


def rmsnorm_bf16(x, g, layer):
    m, d = x.shape
    tm = _pick(m, (512, 256, 128, 8))
    return pl.pallas_call(
        _rmsnorm_kernel,
        grid=(m // tm,),
        in_specs=[pl.BlockSpec((tm, d), lambda i: (i, 0)),
                  pl.BlockSpec((None, 1, d), lambda i: (layer, 0, 0))],
        out_specs=pl.BlockSpec((tm, d), lambda i: (i, 0)),
        out_shape=jax.ShapeDtypeStruct((m, d), BF16),
        compiler_params=_params("parallel"),
        name="rmsnorm",
    )(x, g)


def _cast_kernel(w_ref, o_ref):
    o_ref[...] = w_ref[...].astype(o_ref.dtype)


def cast_layer_bf16(w, layer):
    _, r, c = w.shape
    rows = _pick(r, (128, 64, 32, 16))
    return pl.pallas_call(
        _cast_kernel,
        grid=(r // rows,),
        in_specs=[pl.BlockSpec((None, rows, c), lambda i: (layer, i, 0))],
        out_specs=pl.BlockSpec((None, rows, c), lambda i: (0, i, 0)),
        out_shape=jax.ShapeDtypeStruct((1, r, c), BF16),
        compiler_params=_params("parallel"),
        name="cast_bf16",
    )(w)


def _cast_rows(total_rows, steps):
    for rows in range(BF16_SUBLANES, total_rows + 1, BF16_SUBLANES):
        if total_rows % rows == 0 and total_rows // rows <= steps:
            return rows
    raise ValueError(f"cannot cover {total_rows} rows in {steps} steps")


def _grid_call(body, args, *, grid, in_specs, out_spec, out_shape, name, cast=None,
               vmem_limit=VMEM_LIMIT_BYTES):
    if cast is None:
        return _pipelined_call(body, args, grid=grid, in_specs=in_specs, out_specs=[out_spec],
                               out_shape=[out_shape], name=name, vmem_limit=vmem_limit)[0]
    src, layer = cast
    _, r, c = src.shape
    rows = _cast_rows(r, grid[0] * grid[1])
    blk = lambda i, j: jnp.minimum(i * grid[1] + j, r // rows - 1)
    n_in = len(in_specs)

    def body_with_cast(*refs):
        body(*refs[:n_in], refs[n_in + 1])
        refs[n_in + 2][...] = refs[n_in][...].astype(BF16)

    return _pipelined_call(
        body_with_cast, (*args, src), grid=grid,
        in_specs=[*in_specs, pl.BlockSpec((None, rows, c), lambda i, j: (layer, blk(i, j), 0))],
        out_specs=[out_spec, pl.BlockSpec((None, rows, c), lambda i, j: (0, blk(i, j), 0))],
        out_shape=[out_shape, jax.ShapeDtypeStruct((1, r, c), BF16)],
        name=name, vmem_limit=vmem_limit)


def _pipelined_call(body, args, *, grid, in_specs, out_specs, out_shape, name, vmem_limit):
    n_in = len(in_specs)

    def outer(*refs):
        pltpu.emit_pipeline(body, grid=grid, in_specs=in_specs, out_specs=out_specs)(*refs)

    any_spec = pl.BlockSpec(memory_space=pl.ANY)
    return pl.pallas_call(
        outer, in_specs=[any_spec] * n_in, out_specs=[any_spec] * len(out_specs), out_shape=out_shape,
        compiler_params=pltpu.CompilerParams(vmem_limit_bytes=vmem_limit), name=name)(*args)


def _matmul_kernel(x_ref, w_ref, o_ref):
    o_ref[...] = jnp.dot(x_ref[...], w_ref[...], preferred_element_type=F32).astype(o_ref.dtype)


def matmul(x, w, layer, *, col0, n_out, out_dtype, name, cast=None):
    m, k = x.shape
    tm, tn, vmem_limit = _matmul_tiles(m, k, n_out, jnp.dtype(out_dtype).itemsize, col0=col0)
    c0 = col0 // tn
    return _grid_call(
        _matmul_kernel, (x, w),
        grid=(m // tm, n_out // tn),
        in_specs=[pl.BlockSpec((tm, k), lambda i, j: (i, 0)),
                  pl.BlockSpec((None, k, tn), lambda i, j: (layer, 0, j + c0))],
        out_spec=pl.BlockSpec((tm, tn), lambda i, j: (i, j)),
        out_shape=jax.ShapeDtypeStruct((m, n_out), out_dtype),
        name=name, cast=cast, vmem_limit=vmem_limit)


def _swiglu_kernel(x_ref, wg_ref, wu_ref, o_ref):
    x = x_ref[...]
    g = jnp.dot(x, wg_ref[...], preferred_element_type=F32)
    u = jnp.dot(x, wu_ref[...], preferred_element_type=F32)
    o_ref[...] = (g * jax.nn.sigmoid(g) * u).astype(o_ref.dtype)


def swiglu_hidden(x, wg, wu, layer, cast=None):
    m, k = x.shape
    n = wg.shape[-1]
    tm, tn, vmem_limit = _matmul_tiles(m, k, n, 2, n_weights=2)
    return _grid_call(
        _swiglu_kernel, (x, wg, wu),
        grid=(m // tm, n // tn),
        in_specs=[pl.BlockSpec((tm, k), lambda i, j: (i, 0)),
                  pl.BlockSpec((None, k, tn), lambda i, j: (layer, 0, j)),
                  pl.BlockSpec((None, k, tn), lambda i, j: (layer, 0, j))],
        out_spec=pl.BlockSpec((tm, tn), lambda i, j: (i, j)),
        out_shape=jax.ShapeDtypeStruct((m, n), BF16),
        name="swiglu_hidden", cast=cast, vmem_limit=vmem_limit)


def _norm_residual_kernel(y_ref, h_ref, gpost_ref, gnext_ref, h_out_ref, xn_out_ref):
    h = h_ref[...] + _rms(y_ref[...], gpost_ref[...])
    h_out_ref[...] = h
    xn_out_ref[...] = _rms(h, gnext_ref[...]).astype(xn_out_ref.dtype)


def norm_residual(y, h, g_post, g_next, layer, next_layer):
    m, d = h.shape
    tm = _pick(m, (256, 128, 8))
    row = pl.BlockSpec((tm, d), lambda i: (i, 0))
    return pl.pallas_call(
        _norm_residual_kernel,
        grid=(m // tm,),
        in_specs=[row, row,
                  pl.BlockSpec((None, 1, d), lambda i: (layer, 0, 0)),
                  pl.BlockSpec((None, 1, d), lambda i: (next_layer, 0, 0))],
        out_specs=[row, row],
        out_shape=[jax.ShapeDtypeStruct((m, d), F32), jax.ShapeDtypeStruct((m, d), BF16)],
        compiler_params=_params("parallel"),
        name="norm_residual",
    )(y, h, g_post, g_next)


def _sb_scores(q, k, q0, k0):
    z = lax.dot_general(q, k, (((1,), (1,)), ((), ())), preferred_element_type=F32) * HEAD_DIM ** -0.5
    row = lax.broadcasted_iota(jnp.int32, z.shape, 0)
    col = lax.broadcasted_iota(jnp.int32, z.shape, 1)
    causal = (col - row) < (q0 - k0)
    log_beta = jnp.minimum(z, 0.0) - jnp.log(1.0 + jnp.exp(-jnp.abs(z)))
    log_stay = jnp.where(causal, log_beta - z, 0.0)
    return log_beta, log_stay, causal


def _suffix_sums(tiles, tri2):
    t = ATT_TILE
    x = jnp.concatenate(tiles, axis=0) if len(tiles) > 1 else tiles[0]
    hi = x.astype(BF16)
    lo = (x - hi.astype(F32)).astype(BF16)
    sums = jnp.dot(jnp.concatenate([hi, lo], axis=1), tri2, preferred_element_type=F32)
    return [(sums[n * t:(n + 1) * t, :t], sums[n * t:(n + 1) * t, t:]) for n in range(len(tiles))]


def _sb_attention_kernel(q_ref, k_ref, v_ref, tri2_ref, o_ref):
    s_len = q_ref.shape[0]
    t = ATT_TILE
    nw = SB_WINDOW_TILES
    wk = nw * t

    def windows(tiles):
        pre = []
        for i in tiles:
            q0 = pl.multiple_of(i * t, t)
            first = jnp.maximum(i - (nw - 1), 0)
            k0 = pl.multiple_of(first * t, t)
            q = q_ref[pl.ds(q0, t), :]
            pre.append((q, q0, first, k0) + _sb_scores(q, k_ref[pl.ds(k0, wk), :], q0, k0))
        sums = []
        for c in range(0, len(pre), SB_Q_TILES_PER_SUFFIX_DOT):
            part = pre[c:c + SB_Q_TILES_PER_SUFFIX_DOT]
            sums += _suffix_sums([p[5][:, d * t:(d + 1) * t] for p in part for d in range(nw)], tri2_ref[...])
        out = []
        for u, (q, q0, first, k0, log_beta, _, causal) in enumerate(pre):
            after, run = [], jnp.zeros((t, t), F32)
            for d in reversed(range(nw)):
                cum, tot = sums[u * nw + d]
                after.append(cum + run)
                run = run + tot
            after = jnp.concatenate(after[::-1], axis=1)
            w = jnp.where(causal, jnp.exp(log_beta + after), 0.0)
            acc = jnp.dot(w.astype(BF16), v_ref[pl.ds(k0, wk), :], preferred_element_type=F32)
            out.append((q, q0, first - 1, run, acc))
        return out

    def more(j, run):
        return jnp.logical_and(j >= 0, jnp.max(run) > -F32_EXP_UNDERFLOW)

    def tail(q, q0, j, run, acc):
        def k_tile(state):
            j, _, run, acc = state
            k0 = pl.multiple_of(j * t, t)
            log_beta, log_stay, _ = _sb_scores(q, k_ref[pl.ds(k0, t), :], q0, k0)
            (cum, tot), = _suffix_sums([log_stay], tri2_ref[...])
            w = jnp.exp(log_beta + run + cum)
            acc = acc + jnp.dot(w.astype(BF16), v_ref[pl.ds(k0, t), :], preferred_element_type=F32)
            run = run + tot
            return j - 1, more(j - 1, run), run, acc

        return lax.while_loop(lambda st: st[1], k_tile, (j, more(j, run), run, acc))[3]

    def q_group(g, carry):
        wins = windows([g * SB_Q_TILES_PER_STEP + u for u in range(SB_Q_TILES_PER_STEP)])
        for q, q0, j, run, acc in wins:
            o_ref[pl.ds(q0, t), :] = tail(q, q0, j, run, acc).astype(o_ref.dtype)
        return carry

    lax.fori_loop(0, s_len // (t * SB_Q_TILES_PER_STEP), q_group, 0)


def _suffix_sum_matrix():
    r = jnp.arange(ATT_TILE)
    strict_lower = (r[:, None] > r[None, :]).astype(BF16)
    half = jnp.concatenate([strict_lower, jnp.ones((ATT_TILE, ATT_TILE), BF16)], axis=1)
    return jnp.concatenate([half, half], axis=0)


def sb_attention(qkv, batch, seq):
    m = qkv.shape[0]
    h = N_HEADS_SB
    assert seq % (ATT_TILE * SB_Q_TILES_PER_STEP) == 0 and seq >= SB_WINDOW_TILES * ATT_TILE
    blk = lambda off: pl.BlockSpec((seq, HEAD_DIM), lambda b, n: (b, n + off))
    tri2 = _suffix_sum_matrix()
    return pl.pallas_call(
        _sb_attention_kernel,
        grid=(batch, h),
        in_specs=[blk(0), blk(h), blk(2 * h), pl.BlockSpec(tri2.shape, lambda b, n: (0, 0))],
        out_specs=blk(0),
        out_shape=jax.ShapeDtypeStruct((m, W_SB), BF16),
        compiler_params=_params("parallel", "parallel"),
        name="sb_attention",
    )(qkv, qkv, qkv, tri2)


def _dil_attention_kernel(q_ref, kp_ref, kc_ref, vp_ref, vc_ref, o_ref, lse_ref):
    t = ATT_TILE
    i = pl.program_id(1)
    in_band, own_tile, dist = _dil_band(1)
    lane = lax.broadcasted_iota(jnp.int32, (t, HEAD_DIM), 1)
    for a in range(q_ref.shape[0] // t):
        rows = slice(a * t, (a + 1) * t)
        prev = slice((a - 1) * t, a * t)
        valid = in_band if a else jnp.logical_and(in_band, jnp.logical_or(own_tile, i > 0))
        lse = jnp.zeros((t, HEAD_DIM), F32)
        for hh in range(N_HEADS_PER_DIL):
            sl = slice(hh * HEAD_DIM, (hh + 1) * HEAD_DIM)
            k = jnp.concatenate([kc_ref[prev, sl] if a else kp_ref[:, sl], kc_ref[rows, sl]], axis=0)
            v = jnp.concatenate([vc_ref[prev, sl] if a else vp_ref[:, sl], vc_ref[rows, sl]], axis=0)
            out, lse_h = _dil_tile(q_ref[rows, sl], k, v, valid, dist, hh)
            o_ref[rows, sl] = out.astype(o_ref.dtype)
            lse = jnp.where(lane == hh, lse_h, lse)
        lse_ref[rows, :] = lse


def _dil_tile(q, k, v, valid, dist, head):
    slope = 2.0 ** (-8.0 * (head + 1) / N_HEADS_PER_DIL)
    s = lax.dot_general(q, k, (((1,), (1,)), ((), ())), preferred_element_type=F32) * HEAD_DIM ** -0.5
    s = jnp.where(valid, s - slope * dist, -jnp.inf)
    mx = jnp.max(s, axis=1, keepdims=True)
    e = jnp.exp(s - mx).astype(BF16)
    v_ones = jnp.concatenate([v, jnp.ones_like(v)], axis=1)
    pv_den = jnp.dot(e, v_ones, preferred_element_type=F32)
    pv, den = pv_den[:, :HEAD_DIM], pv_den[:, HEAD_DIM:]
    return pv / den, mx + jnp.log(den[:, :1])


def _dil_band(dilation):
    t = ATT_TILE
    row = lax.broadcasted_iota(jnp.int32, (t, 2 * t), 0)
    col = lax.broadcasted_iota(jnp.int32, (t, 2 * t), 1)
    steps = t + row - col
    return jnp.logical_and(steps >= 0, steps <= t), col >= t, (steps * dilation).astype(F32)


def _dil_strided_kernel(q_ref, kp_ref, kc_ref, vp_ref, vc_ref, o_ref, lse_ref,
                        qw, kpw, kcw, vpw, vcw, ow, lse_rows, *, dilation):
    t = ATT_TILE
    r = dilation
    half = r // 2
    heads = N_HEADS_PER_DIL
    i = pl.program_id(1)
    for src, words in ((q_ref, qw), (kp_ref, kpw), (kc_ref, kcw), (vp_ref, vpw), (vc_ref, vcw)):
        for hh in range(heads):
            words[hh] = pltpu.bitcast(src[:, hh * HEAD_DIM:(hh + 1) * HEAD_DIM], jnp.uint32)
    in_band, own_tile, dist = _dil_band(r)
    valid = jnp.logical_and(in_band, jnp.logical_or(own_tile, i > 0))
    lane = lax.broadcasted_iota(jnp.int32, (t, HEAD_DIM), 1)
    high = jnp.uint32(0xFFFF0000)

    def unpack(words, odd):
        return pltpu.bitcast(words & high if odd else words << 16, F32).astype(BF16)

    def bf16_bits(x):
        return pltpu.bitcast(x.astype(BF16).astype(F32), jnp.uint32)

    def class_pair(p, carry):
        sel = pl.ds(p, t, stride=half)
        lses = [jnp.zeros((t, HEAD_DIM), F32)] * 2
        for hh in range(heads):
            wq, wkp, wkc, wvp, wvc = (w[hh, sel, :] for w in (qw, kpw, kcw, vpw, vcw))
            outs = []
            for odd in (0, 1):
                k = jnp.concatenate([unpack(wkp, odd), unpack(wkc, odd)], axis=0)
                v = jnp.concatenate([unpack(wvp, odd), unpack(wvc, odd)], axis=0)
                out, lse_h = _dil_tile(unpack(wq, odd), k, v, valid, dist, hh)
                outs.append(out)
                lses[odd] = jnp.where(lane == hh, lse_h, lses[odd])
            ow[hh, sel, :] = (bf16_bits(outs[0]) >> 16) | (bf16_bits(outs[1]) & high)
        for odd in (0, 1):
            lse_rows[pl.ds(2 * p + odd, t, stride=r), :] = lses[odd]
        return carry

    lax.fori_loop(0, half, class_pair, 0)
    for hh in range(heads):
        o_ref[:, hh * HEAD_DIM:(hh + 1) * HEAD_DIM] = pltpu.bitcast(ow[hh], BF16)
    lse_ref[...] = lse_rows[...]


def dil_attention_group(qkv, batch, seq, group):
    m, wq = qkv.shape
    _, r = DIL_CONFIGS[group]
    t = ATT_TILE
    w = W_DIL_GROUP
    assert wq % w == 0 and seq % (t * r) == 0
    q_col = (3 * W_SB) // w + group
    k_col = q_col + W_DIL // w
    v_col = k_col + W_DIL // w
    out_shape = [jax.ShapeDtypeStruct((m, w), BF16), jax.ShapeDtypeStruct((m, HEAD_DIM), F32)]

    if r > 1:
        assert r % 2 == 0
        rows = t * r
        nb = seq // rows
        cur = lambda col: pl.BlockSpec((rows, w), lambda b, i: (b * nb + i, col))
        prev = lambda col: pl.BlockSpec((rows, w), lambda b, i: (b * nb + jnp.maximum(i - 1, 0), col))
        words = pltpu.VMEM((N_HEADS_PER_DIL, rows // 2, HEAD_DIM), jnp.uint32)
        return pl.pallas_call(
            functools.partial(_dil_strided_kernel, dilation=r),
            grid=(batch, nb),
            in_specs=[cur(q_col), prev(k_col), cur(k_col), prev(v_col), cur(v_col)],
            out_specs=[pl.BlockSpec((rows, w), lambda b, i: (b * nb + i, 0)),
                       pl.BlockSpec((rows, HEAD_DIM), lambda b, i: (b * nb + i, 0))],
            out_shape=out_shape,
            scratch_shapes=[words] * 6 + [pltpu.VMEM((rows, HEAD_DIM), F32)],
            compiler_params=_params("parallel", "arbitrary"),
            name=f"dil_attention_r{r}",
        )(qkv, qkv, qkv, qkv, qkv)

    nb = seq // t
    tiles = _pick(nb, (DIL_TILES_PER_STEP, 1))
    ns = nb // tiles

    def spec(col, prev):
        if prev:
            return pl.BlockSpec((t, w), lambda b, i: (b * nb + jnp.maximum(i * tiles - 1, 0), col))
        return pl.BlockSpec((tiles * t, w), lambda b, i: (b * ns + i, col))

    out_spec = lambda width: pl.BlockSpec((tiles * t, width), lambda b, i: (b * ns + i, 0))
    return pl.pallas_call(
        _dil_attention_kernel,
        grid=(batch, ns),
        in_specs=[spec(q_col, False), spec(k_col, True), spec(k_col, False),
                  spec(v_col, True), spec(v_col, False)],
        out_specs=[out_spec(w), out_spec(HEAD_DIM)],
        out_shape=out_shape,
        compiler_params=_params("parallel", "arbitrary"),
        name="dil_attention_r1",
    )(qkv, qkv, qkv, qkv, qkv)


def _dil_merge_kernel(*refs):
    n = len(DIL_CONFIGS)
    o_refs, lse_refs, out_ref = refs[:n], refs[n:2 * n], refs[2 * n]
    lses = [r[...] for r in lse_refs]
    mx = functools.reduce(jnp.maximum, lses)
    ws = [jnp.exp(l - mx) for l in lses]
    total = sum(ws)
    alphas = [w / total for w in ws]
    for hh in range(N_HEADS_PER_DIL):
        sl = slice(hh * HEAD_DIM, (hh + 1) * HEAD_DIM)
        merged = sum(a[:, hh:hh + 1] * o[:, sl] for a, o in zip(alphas, o_refs))
        out_ref[:, sl] = merged.astype(out_ref.dtype)


def dil_merge(outs, lses):
    m, w = outs[0].shape
    tm = _pick(m, (1024, 512, 256, 128, 8))
    row = pl.BlockSpec((tm, w), lambda i: (i, 0))
    lse_row = pl.BlockSpec((tm, HEAD_DIM), lambda i: (i, 0))
    return pl.pallas_call(
        _dil_merge_kernel,
        grid=(m // tm,),
        in_specs=[row] * len(outs) + [lse_row] * len(lses),
        out_specs=row,
        out_shape=jax.ShapeDtypeStruct((m, w), BF16),
        compiler_params=_params("parallel"),
        name="dil_merge",
    )(*outs, *lses)


def _gated_proj_kernel(x_ref, wgs_ref, wgd_ref, osb_ref, od_ref, psb_ref, pd_ref, o_ref):
    x = x_ref[...]
    gate_sb = jax.nn.sigmoid(jnp.dot(x, wgs_ref[...], preferred_element_type=F32))
    gate_d = jax.nn.sigmoid(jnp.dot(x, wgd_ref[...], preferred_element_type=F32))
    a = jnp.dot(osb_ref[...], psb_ref[...], preferred_element_type=F32)
    b = jnp.dot(od_ref[...], pd_ref[...], preferred_element_type=F32)
    o_ref[...] = (gate_sb * a + gate_d * b).astype(o_ref.dtype)


def gated_proj(xn, w_in, w_layer, o_sb, o_d, p_sb, p_d, layer, cast=None):
    m, k = xn.shape
    d = p_sb.shape[-1]
    col_sb = w_in.shape[-1] - 2 * d
    tm = _pick(m, (1024, 512, 256, 128))
    tn = next(c for c in (512, 256, 128) if d % c == 0 and col_sb % c == 0)
    c_sb, c_d = col_sb // tn, (col_sb + d) // tn
    return _grid_call(
        _gated_proj_kernel, (xn, w_in, w_in, o_sb, o_d, p_sb, p_d),
        grid=(m // tm, d // tn),
        in_specs=[pl.BlockSpec((tm, k), lambda i, j: (i, 0)),
                  pl.BlockSpec((None, k, tn), lambda i, j: (w_layer, 0, j + c_sb)),
                  pl.BlockSpec((None, k, tn), lambda i, j: (w_layer, 0, j + c_d)),
                  pl.BlockSpec((tm, o_sb.shape[1]), lambda i, j: (i, 0)),
                  pl.BlockSpec((tm, o_d.shape[1]), lambda i, j: (i, 0)),
                  pl.BlockSpec((None, p_sb.shape[1], tn), lambda i, j: (layer, 0, j)),
                  pl.BlockSpec((None, p_d.shape[1], tn), lambda i, j: (layer, 0, j))],
        out_spec=pl.BlockSpec((tm, tn), lambda i, j: (i, j)),
        out_shape=jax.ShapeDtypeStruct((m, d), BF16),
        name="gated_proj", cast=cast)


def _ffn_close_ple_kernel(y_ref, p_ref, h_ref, gd_ref, gu_ref, wp_ref,
                          g_ffn_post_ref, g_gate_ref, g_post_ref, g_next_ref, h_out_ref, xn_out_ref):
    h = h_ref[...] + _rms(y_ref[...], g_ffn_post_ref[...])
    xn = _rms(h, g_gate_ref[...]).astype(BF16)
    low = jnp.dot(xn, gd_ref[...], preferred_element_type=F32)
    gate = jax.nn.sigmoid(jnp.dot(low.astype(BF16), gu_ref[...], preferred_element_type=F32))
    e = jnp.dot(p_ref[...].astype(BF16), wp_ref[...], preferred_element_type=F32) * gate
    h = h + _rms(e, g_post_ref[...])
    h_out_ref[...] = h
    xn_out_ref[...] = _rms(h, g_next_ref[...]).astype(xn_out_ref.dtype)


def ffn_close_ple_block(y, p, h, g_down, g_up, w_ple, g_ffn_post, g_gate, g_post, g_next,
                        layer, next_layer):
    m, d = h.shape
    dp = p.shape[-1]
    tm = _pick(m, (256, 128, 8))
    row = pl.BlockSpec((tm, d), lambda i: (i, 0))
    gain = lambda l: pl.BlockSpec((None, 1, d), lambda i: (l, 0, 0))
    return pl.pallas_call(
        _ffn_close_ple_kernel,
        grid=(m // tm,),
        in_specs=[row,
                  pl.BlockSpec((None, tm, dp), lambda i: (layer, i, 0)),
                  row,
                  pl.BlockSpec((None, d, dp), lambda i: (layer, 0, 0)),
                  pl.BlockSpec((None, dp, d), lambda i: (layer, 0, 0)),
                  pl.BlockSpec((None, dp, d), lambda i: (layer, 0, 0)),
                  gain(layer), gain(layer), gain(layer), gain(next_layer)],
        out_specs=[row, row],
        out_shape=[jax.ShapeDtypeStruct((m, d), F32), jax.ShapeDtypeStruct((m, d), BF16)],
        compiler_params=_params("parallel"),
        name="ffn_close_ple",
    )(y, p, h, g_down, g_up, w_ple, g_ffn_post, g_gate, g_post, g_next)


def kernel(x, p, w_in, w_proj_sb, w_proj_dil, w_out, g_mix_pre, g_mix_post,
           w_ffn_gate, w_ffn_up, w_ffn_down, g_ffn_pre, g_ffn_post,
           w_ple_in, w_ple_gate_down, w_ple_gate_up, g_ple_gate, g_ple_post):
    b, s, d = x.shape
    depth = w_in.shape[0]
    m = b * s
    assert w_in.shape[-1] == W_QKV + 2 * d

    bf = lambda w: w.astype(BF16)
    w_in_bf = cast_layer_bf16(w_in, 0)
    w_proj_sb, w_proj_dil = bf(w_proj_sb), bf(w_proj_dil)
    w_ple_in, w_ple_gate_down, w_ple_gate_up = bf(w_ple_in), bf(w_ple_gate_down), bf(w_ple_gate_up)
    gains = lambda g: g.reshape(depth, 1, d)
    g_mix_pre, g_mix_post, g_ffn_pre, g_ffn_post, g_ple_gate, g_ple_post = map(
        gains, (g_mix_pre, g_mix_post, g_ffn_pre, g_ffn_post, g_ple_gate, g_ple_post))
    p = p.reshape(depth, m, p.shape[-1])

    h = x.reshape(m, d)
    xn = rmsnorm_bf16(h, g_mix_pre, 0)
    for i in range(depth):
        qkv, w_out_bf = matmul(xn, w_in_bf, 0, col0=0, n_out=W_QKV, out_dtype=BF16,
                               name="in_proj_qkv", cast=(w_out, i))
        o_sb = sb_attention(qkv, b, s)
        groups = [dil_attention_group(qkv, b, s, g) for g in range(len(DIL_CONFIGS))]
        o_d = dil_merge([o for o, _ in groups], [l for _, l in groups])
        merged, w_gate_bf = gated_proj(xn, w_in_bf, 0, o_sb, o_d, w_proj_sb, w_proj_dil, i,
                                       cast=(w_ffn_gate, i))
        y, w_up_bf = matmul(merged, w_out_bf, 0, col0=0, n_out=d, out_dtype=F32, name="out_proj",
                            cast=(w_ffn_up, i))
        h, xn = norm_residual(y, h, g_mix_post, g_ffn_pre, i, i)
        hidden, w_down_bf = swiglu_hidden(xn, w_gate_bf, w_up_bf, 0, cast=(w_ffn_down, i))
        if i + 1 < depth:
            y, w_in_bf = matmul(hidden, w_down_bf, 0, col0=0, n_out=d, out_dtype=F32,
                                name="ffn_down", cast=(w_in, i + 1))
        else:
            y = matmul(hidden, w_down_bf, 0, col0=0, n_out=d, out_dtype=F32, name="ffn_down")
        h, xn = ffn_close_ple_block(y, p, h, w_ple_gate_down, w_ple_gate_up, w_ple_in,
                                    g_ffn_post, g_ple_gate, g_ple_post, g_mix_pre, i, (i + 1) % depth)
    return h.reshape(b, s, d)
```
